```python
import jax
import jax.numpy as jnp
from jax import lax
import numpy as np


D_MODEL = 2048
BATCH = 4
SEQ = 4096
DEPTH = 1

GRID_W = 64
CTX_LEN = 256
EPS = 1e-6

ATT_HEADS = 16
ATT_KV_HEADS = 4
ATT_GROUP = ATT_HEADS // ATT_KV_HEADS
ATT_HEAD_DIM = 128
ATT_WIDTH = ATT_HEADS * ATT_HEAD_DIM
ATT_KV_WIDTH = ATT_KV_HEADS * ATT_HEAD_DIM
WINDOW = 128
Q_BLOCK = 128
ROPE_BASE = 10000.0

HG_HEADS = 16
HG_EXPAND = 128
HG_HEAD_V = 128
HG_FDIM = HG_HEADS * HG_EXPAND
HG_VDIM = HG_HEADS * HG_HEAD_V
CHUNK = 64
LB_REST_INIT = 1.5

N_BRANCH = 2

OFF_ATT_Q = 0
OFF_ATT_K = OFF_ATT_Q + ATT_WIDTH
OFF_ATT_V = OFF_ATT_K + ATT_KV_WIDTH
OFF_ATT_G = OFF_ATT_V + ATT_KV_WIDTH
OFF_HG_Q = OFF_ATT_G + ATT_WIDTH
OFF_HG_FF = OFF_HG_Q + HG_FDIM
OFF_HG_FB = OFF_HG_FF + HG_FDIM
OFF_HG_I = OFF_HG_FB + HG_FDIM
OFF_HG_G = OFF_HG_I + HG_VDIM
OFF_MERGE = OFF_HG_G + HG_VDIM
IN_COLS = OFF_MERGE + N_BRANCH * D_MODEL

kernel_name = 'hybrid_hgrn2_swa_sink_block'


def rmsnorm(x, gain):
    xf = x.astype(jnp.float32)
    y = xf * lax.rsqrt(jnp.mean(xf * xf, axis=-1, keepdims=True) + EPS)
    return (y * gain.astype(jnp.float32)).astype(x.dtype)


def split_columns(p):
    return (p[..., OFF_ATT_Q:OFF_ATT_K], p[..., OFF_ATT_K:OFF_ATT_V], p[..., OFF_ATT_V:OFF_ATT_G],
            p[..., OFF_ATT_G:OFF_HG_Q], p[..., OFF_HG_Q:OFF_HG_FF], p[..., OFF_HG_FF:OFF_HG_FB],
            p[..., OFF_HG_FB:OFF_HG_I], p[..., OFF_HG_I:OFF_HG_G], p[..., OFF_HG_G:OFF_MERGE],
            p[..., OFF_MERGE:OFF_MERGE + D_MODEL], p[..., OFF_MERGE + D_MODEL:IN_COLS])


def axial_rope_tables(rows, dtype):
    row = jnp.repeat(jnp.arange(rows, dtype=jnp.float32), GRID_W)
    col = jnp.tile(jnp.arange(GRID_W, dtype=jnp.float32), rows)
    half = ATT_HEAD_DIM // 2
    inv_freq = 1.0 / (ROPE_BASE ** (jnp.arange(0, half, 2, dtype=jnp.float32) / half))
    ang_r = row[:, None] * inv_freq[None, :]
    ang_c = col[:, None] * inv_freq[None, :]
    ang = jnp.concatenate([ang_r, ang_r, ang_c, ang_c], axis=-1)
    return jnp.cos(ang).astype(dtype), jnp.sin(ang).astype(dtype)


def apply_axial_rope(x, cos, sin):
    x1, x2, x3, x4 = jnp.split(x, 4, axis=-1)
    rot = jnp.concatenate([-x2, x1, -x4, x3], axis=-1)
    return x * cos[:, None, :] + rot * sin[:, None, :]


def sink_softmax_av(s, v, sink):
    sk = sink.astype(jnp.float32).reshape(ATT_KV_HEADS, ATT_GROUP)[None, :, :, None]
    m = jnp.maximum(jnp.max(s, axis=-1), sk)
    p = jnp.exp(s - m[..., None])
    den = jnp.sum(p, axis=-1) + jnp.exp(sk - m)
    p = (p / den[..., None]).astype(v.dtype)
    return jnp.einsum('bkgqn,bnkd->bqkgd', p, v)


def window_attention(q, k, v, k_ctx, v_ctx, sink):
    B, n = q.shape[0], q.shape[1]
    nb = n // Q_BLOCK
    span = Q_BLOCK + 2 * WINDOW
    qb = q.reshape(B, nb, Q_BLOCK, ATT_KV_HEADS, ATT_GROUP, ATT_HEAD_DIM).transpose(1, 0, 2, 3, 4, 5)
    pad = ((0, 0), (WINDOW, WINDOW), (0, 0), (0, 0))
    kp = jnp.pad(k, pad)
    vp = jnp.pad(v, pad)

    def block(args):
        qn, start = args
        kb = lax.dynamic_slice_in_dim(kp, start, span, axis=1)
        vb = lax.dynamic_slice_in_dim(vp, start, span, axis=1)
        qi = start + jnp.arange(Q_BLOCK)
        kj = start - WINDOW + jnp.arange(span)
        valid = (jnp.abs(qi[:, None] - kj[None, :]) <= WINDOW) & (kj >= 0)[None, :] & (kj < n)[None, :]
        s_lat = jnp.einsum('bqkgd,bskd->bkgqs', qn, kb).astype(jnp.float32)
        s_lat = jnp.where(valid, s_lat, -jnp.inf)
        s_ctx = jnp.einsum('bqkgd,bskd->bkgqs', qn, k_ctx).astype(jnp.float32)
        s = jnp.concatenate([s_lat, s_ctx], axis=-1)
        return sink_softmax_av(s, jnp.concatenate([vb, v_ctx], axis=1), sink)

    out = lax.map(block, (qb, jnp.arange(nb) * Q_BLOCK))
    return out.transpose(1, 0, 2, 3, 4, 5).reshape(B, n, ATT_WIDTH)


def context_attention(q_c, k_c, v_c, sink):
    B, L = q_c.shape[0], q_c.shape[1]
    qg = q_c.reshape(B, L, ATT_KV_HEADS, ATT_GROUP, ATT_HEAD_DIM)
    s = jnp.einsum('bqkgd,bskd->bkgqs', qg, k_c).astype(jnp.float32)
    return sink_softmax_av(s, v_c, sink).reshape(B, L, ATT_WIDTH)


def to_heads(a):
    B, T, _ = a.shape
    return a.reshape(B, T, HG_HEADS, -1).transpose(0, 2, 1, 3)


def hgrn_forget(z, lb):
    lbf = lb[None, None, :]
    f = lbf + (1.0 - lbf) * jax.nn.sigmoid(z.astype(jnp.float32))
    return to_heads(jnp.log(f)), to_heads(1.0 - f)


def hgrn2_scan(q, k, v, logf, s0):
    B, H, T, dk = q.shape
    dv = v.shape[-1]
    nc = T // CHUNK

    def chunks(a):
        return jnp.moveaxis(a.astype(jnp.float32).reshape(B, H, nc, CHUNK, a.shape[-1]), 2, 0)

    causal = jnp.tril(jnp.ones((CHUNK, CHUNK), dtype=bool))

    def step(S, inp):
        qc, kc, vc, gc = inp
        G = jnp.cumsum(gc, axis=2)
        G_end = G[:, :, -1:, :]
        o_inter = jnp.einsum('bhtk,bhkv->bhtv', qc * jnp.exp(G), S)
        diff = G[:, :, :, None, :] - G[:, :, None, :, :]
        decay = jnp.exp(jnp.where(causal[:, :, None], diff, -jnp.inf))
        A = jnp.einsum('bhtk,bhsk,bhtsk->bhts', qc, kc, decay)
        o = o_inter + jnp.einsum('bhts,bhsv->bhtv', A, vc)
        S_new = jnp.exp(G_end[:, :, 0, :, None]) * S + jnp.einsum('bhsk,bhsv->bhkv', kc * jnp.exp(G_end - G), vc)
        return S_new, o

    s_end, o = lax.scan(step, s0.astype(jnp.float32), (chunks(q), chunks(k), chunks(v), chunks(logf)))
    o = jnp.moveaxis(o, 0, 2).reshape(B, H, T, dv)
    return o, s_end


def hgrn2_final_state(k, v, logf):
    G = jnp.cumsum(logf, axis=2)
    w = k.astype(jnp.float32) * jnp.exp(G[:, :, -1:, :] - G)
    return jnp.einsum('bhsk,bhsv->bhkv', w, v.astype(jnp.float32))


def flip_t(a):
    return jnp.flip(a, axis=2)


def hgrn2_bidir(q, v, k_f, g_f, k_b, g_b, s_f, s_b):
    o_f, s_f_end = hgrn2_scan(q, k_f, v, g_f, s_f)
    o_b, s_b_end = hgrn2_scan(flip_t(q), flip_t(k_b), flip_t(v), flip_t(g_b), s_b)
    return o_f + flip_t(o_b), s_f_end, s_b_end


def hgrn_readout(o, gain, gate):
    B, H, T, dv = o.shape
    on = rmsnorm(o.transpose(0, 2, 1, 3), gain.reshape(H, dv)).reshape(B, T, H * dv)
    return on.astype(gate.dtype) * jax.nn.silu(gate)


def merge_and_project(att_branch, hg_branch, m_a, m_b, b_m, w_o_hgrn_l, w_o_attn_l, w_out_l):
    y_a = hg_branch @ w_o_hgrn_l
    y_b = att_branch @ w_o_attn_l
    y = jax.nn.sigmoid(m_a + b_m[0]) * y_a + jax.nn.sigmoid(m_b + b_m[1]) * y_b
    return y @ w_out_l


def setup_inputs(seed: int = 0) -> dict:
    key = jax.random.key(seed)
    ks = jax.random.split(key, 17)
    f32 = jnp.float32
    nrm = lambda k, shape: jax.random.normal(k, shape, dtype=f32)
    return {
        'x': nrm(ks[0], (BATCH, SEQ, D_MODEL)),
        'c': nrm(ks[1], (BATCH, D_MODEL)),
        'ctx': nrm(ks[2], (BATCH, CTX_LEN, D_MODEL)),
        'c_ctx': nrm(ks[3], (D_MODEL,)),
        'w_ada': nrm(ks[4], (DEPTH, D_MODEL, 3 * D_MODEL)) * (0.5 * D_MODEL ** -0.5),
        'b_ada': nrm(ks[5], (DEPTH, 3 * D_MODEL)) * 0.02,
        'norm_gain': 1.0 + 0.02 * nrm(ks[6], (DEPTH, D_MODEL)),
        'w_in': nrm(ks[7], (DEPTH, D_MODEL, IN_COLS)) * D_MODEL ** -0.5,
        'b_merge': nrm(ks[8], (DEPTH, N_BRANCH, D_MODEL)) * 0.02,
        'lb_logits_fwd': (0.1 * nrm(ks[9], (DEPTH + 1, HG_FDIM))).at[-1].add(LB_REST_INIT),
        'lb_logits_bwd': (0.1 * nrm(ks[10], (DEPTH + 1, HG_FDIM))).at[-1].add(LB_REST_INIT),
        'hgrn_norm_gain': 1.0 + 0.02 * nrm(ks[11], (DEPTH, HG_VDIM)),
        'w_o_hgrn': nrm(ks[12], (DEPTH, HG_VDIM, D_MODEL)) * HG_VDIM ** -0.5,
        'sink_logits': nrm(ks[13], (DEPTH, ATT_HEADS)) * 0.5,
        'w_o_attn': nrm(ks[14], (DEPTH, ATT_WIDTH, D_MODEL)) * ATT_WIDTH ** -0.5,
        'w_out': nrm(ks[15], (DEPTH, D_MODEL, D_MODEL)) * D_MODEL ** -0.5,
        'final_norm_gain': 1.0 + 0.02 * nrm(ks[16], (D_MODEL,)),
    }


def reference(x, c, ctx, c_ctx, w_ada, b_ada, norm_gain, w_in, b_merge, lb_logits_fwd, lb_logits_bwd,
              hgrn_norm_gain, w_o_hgrn, sink_logits, w_o_attn, w_out, final_norm_gain):
    B, n, _ = x.shape
    L = ctx.shape[1]
    rows = n // GRID_W
    cos, sin = axial_rope_tables(rows, x.dtype)
    q_scale = ATT_HEAD_DIM ** -0.5
    lb_f_all = jnp.cumsum(jax.nn.softmax(lb_logits_fwd.astype(jnp.float32), axis=0), axis=0)
    lb_b_all = jnp.cumsum(jax.nn.softmax(lb_logits_bwd.astype(jnp.float32), axis=0), axis=0)
    h, hc = x, ctx
    for l in range(DEPTH):
        last = l == DEPTH - 1
        mod = jax.nn.silu(c) @ w_ada[l] + b_ada[l]
        shift, scale, gate = jnp.split(mod[:, None, :], 3, axis=-1)
        mod_c = jax.nn.silu(c_ctx) @ w_ada[l] + b_ada[l]
        shift_c, scale_c, gate_c = jnp.split(mod_c, 3, axis=-1)
        xn = rmsnorm(h, norm_gain[l]) * (1.0 + scale) + shift
        cn = rmsnorm(hc, norm_gain[l]) * (1.0 + scale_c) + shift_c
        aq, ak, av, ag, hq, hff, hfb, hi, hg, ma, mb = split_columns(xn @ w_in[l])
        caq, cak, cav, cag, chq, chff, chfb, chi, chg, cma, cmb = split_columns(cn @ w_in[l])

        q = apply_axial_rope(aq.reshape(B, n, ATT_HEADS, ATT_HEAD_DIM), cos, sin) * q_scale
        k = apply_axial_rope(ak.reshape(B, n, ATT_KV_HEADS, ATT_HEAD_DIM), cos, sin)
        v = av.reshape(B, n, ATT_KV_HEADS, ATT_HEAD_DIM)
        k_c = cak.reshape(B, L, ATT_KV_HEADS, ATT_HEAD_DIM)
        v_c = cav.reshape(B, L, ATT_KV_HEADS, ATT_HEAD_DIM)
        att_branch = window_attention(q, k, v, k_c, v_c, sink_logits[l]) * jax.nn.silu(ag)

        g_f, k_f = hgrn_forget(hff, lb_f_all[l])
        g_b, k_b = hgrn_forget(hfb, lb_b_all[l])
        cg_f, ck_f = hgrn_forget(chff, lb_f_all[l])
        cg_b, ck_b = hgrn_forget(chfb, lb_b_all[l])
        v_h = to_heads(hi)
        cv_h = to_heads(chi)
        q_h = to_heads(jax.nn.silu(hq))
        if last:
            s_f = hgrn2_final_state(ck_f, cv_h, cg_f)
            s_b = hgrn2_final_state(flip_t(ck_b), flip_t(cv_h), flip_t(cg_b))
        else:
            cq_h = to_heads(jax.nn.silu(chq))
            zero = jnp.zeros((B, HG_HEADS, HG_EXPAND, HG_HEAD_V), jnp.float32)
            co, s_f, s_b = hgrn2_bidir(cq_h, cv_h, ck_f, cg_f, ck_b, cg_b, zero, zero)
            c_att = context_attention(caq.reshape(B, L, ATT_HEADS, ATT_HEAD_DIM) * q_scale, k_c, v_c,
                                      sink_logits[l]) * jax.nn.silu(cag)
            c_hg = hgrn_readout(co, hgrn_norm_gain[l], chg)
            hc = hc + gate_c * merge_and_project(c_att, c_hg, cma, cmb, b_merge[l],
                                                 w_o_hgrn[l], w_o_attn[l], w_out[l])
        o, _, _ = hgrn2_bidir(q_h, v_h, k_f, g_f, k_b, g_b, s_f, s_b)
        hg_branch = hgrn_readout(o, hgrn_norm_gain[l], hg)

        h = h + gate * merge_and_project(att_branch, hg_branch, ma, mb, b_merge[l],
                                         w_o_hgrn[l], w_o_attn[l], w_out[l])
    return rmsnorm(h, final_norm_gain)
```

```python
import functools

import jax
import jax.numpy as jnp
from jax import lax
from jax.experimental import pallas as pl
from jax.experimental.pallas import tpu as pltpu

F32 = jnp.float32
BF16 = jnp.bfloat16

D_MODEL = 2048
GRID_W = 64
EPS = 1e-6
HEAD = 128
ATT_HEADS = 16
ATT_KV_HEADS = 4
ATT_GROUP = ATT_HEADS // ATT_KV_HEADS
WINDOW = 128
ROPE_BASE = 10000.0
HG_HEADS = 16
CHUNK = 64
IN_COLS = 19456

P_Q, P_AG, P_HQ, P_HFF, P_HFB, P_HI, P_HG, P_MA, P_MB, P_K, P_V = (
    0, 2048, 4096, 6144, 8192, 10240, 12288, 14336, 16384, 18432, 18944)
C_K, C_V, C_HFF, C_HFB, C_HI, C_COLS = 0, 512, 1024, 3072, 5120, 7168

VMEM_LIMIT = 56 * 1024 * 1024


def _silu(x):
    return x * jax.nn.sigmoid(x)


def _dot_nt(a, b):
    return lax.dot_general(a, b, (((1,), (1,)), ((), ())), preferred_element_type=F32)


def _dot_tn(a, b):
    return lax.dot_general(a, b, (((0,), (0,)), ((), ())), preferred_element_type=F32)


def _ada_kernel(c_ref, w_ref, b_ref, o_ref):
    s = _silu(c_ref[...])
    o_ref[...] = jnp.dot(s.astype(BF16), w_ref[...].astype(BF16),
                         preferred_element_type=F32) + b_ref[...]


def _ada(c8, w_ada, b_ada):
    n_out = w_ada.shape[1]
    tn = 1024
    return pl.pallas_call(
        _ada_kernel,
        grid=(n_out // tn,),
        in_specs=[pl.BlockSpec((8, D_MODEL), lambda j: (0, 0)),
                  pl.BlockSpec((D_MODEL, tn), lambda j: (0, j)),
                  pl.BlockSpec((1, tn), lambda j: (0, j))],
        out_specs=pl.BlockSpec((8, tn), lambda j: (0, j)),
        out_shape=jax.ShapeDtypeStruct((8, n_out), F32),
        compiler_params=pltpu.CompilerParams(
            dimension_semantics=("arbitrary",), vmem_limit_bytes=VMEM_LIMIT),
        name="ada",
    )(c8, w_ada, b_ada)


def _rope(t, cos, sin_lo, sin_hi):
    return t * cos + pltpu.roll(t, 96, 1) * sin_lo + pltpu.roll(t, 32, 1) * sin_hi


def _inproj_kernel(x_ref, mod_ref, gain_ref, cos_ref, slo_ref, shi_ref, w_ref, o_ref, xn_ref,
                   *, rope, tn):
    j = pl.program_id(1)

    @pl.when(j == 0)
    def _():
        x = x_ref[...]
        y = x * lax.rsqrt(jnp.mean(x * x, axis=-1, keepdims=True) + EPS) * gain_ref[...]
        shift = mod_ref[0, :, 0:D_MODEL]
        scale = mod_ref[0, :, D_MODEL:2 * D_MODEL]
        xn_ref[...] = (y * (1.0 + scale) + shift).astype(BF16)

    acc = jnp.dot(xn_ref[...], w_ref[...], preferred_element_type=F32)
    if not rope:
        o_ref[...] = acc.astype(o_ref.dtype)
        return

    n_heads = tn // HEAD
    q_tiles = P_AG // tn
    kv_tile = P_K // tn
    k_heads = (P_V - P_K) // HEAD
    q_scale = HEAD ** -0.5

    def store_rope(h, mult):
        sl = slice(h * HEAD, (h + 1) * HEAD)
        r = _rope(acc[:, sl], cos_ref[...], slo_ref[...], shi_ref[...])
        if mult is not None:
            r = r * mult
        o_ref[:, sl] = r.astype(o_ref.dtype)

    @pl.when(j < q_tiles)
    def _():
        for h in range(n_heads):
            store_rope(h, q_scale)

    @pl.when(j == kv_tile)
    def _():
        for h in range(k_heads):
            store_rope(h, None)
        o_ref[:, k_heads * HEAD:] = acc[:, k_heads * HEAD:].astype(o_ref.dtype)

    @pl.when(jnp.logical_and(j >= q_tiles, j != kv_tile))
    def _():
        o_ref[...] = acc.astype(o_ref.dtype)


def _inproj_latent(x2, mod3, gain, cos, slo, shi, w_bf, seq):
    m = x2.shape[0]
    tm, tn = 1024, 1024
    tiles_per_seq = seq // tm
    n_tiles = IN_COLS // tn

    def w_map(i, j):
        src = jnp.where(j < 2, j, jnp.where(j == n_tiles - 1, 2, j + 1))
        return (0, src)

    return pl.pallas_call(
        functools.partial(_inproj_kernel, rope=True, tn=tn),
        grid=(m // tm, n_tiles),
        in_specs=[pl.BlockSpec((tm, D_MODEL), lambda i, j: (i, 0)),
                  pl.BlockSpec((1, 1, 3 * D_MODEL), lambda i, j: (i // tiles_per_seq, 0, 0)),
                  pl.BlockSpec((1, D_MODEL), lambda i, j: (0, 0)),
                  pl.BlockSpec((tm, HEAD), lambda i, j: (i % tiles_per_seq, 0)),
                  pl.BlockSpec((tm, HEAD), lambda i, j: (i % tiles_per_seq, 0)),
                  pl.BlockSpec((tm, HEAD), lambda i, j: (i % tiles_per_seq, 0)),
                  pl.BlockSpec((D_MODEL, tn), w_map)],
        out_specs=pl.BlockSpec((tm, tn), lambda i, j: (i, j)),
        out_shape=jax.ShapeDtypeStruct((m, IN_COLS), BF16),
        scratch_shapes=[pltpu.VMEM((tm, D_MODEL), BF16)],
        compiler_params=pltpu.CompilerParams(
            dimension_semantics=("arbitrary", "arbitrary"), vmem_limit_bytes=VMEM_LIMIT),
        name="inproj",
    )(x2, mod3, gain, cos, slo, shi, w_bf)


def _inproj_ctx(ctx2, modc3, gain, cos, slo, shi, w_bf):
    m = ctx2.shape[0]
    tn = 1024
    n_tiles = C_COLS // tn

    def w_map(i, j):
        return (0, jnp.where(j == 0, 2, j + 6))

    return pl.pallas_call(
        functools.partial(_inproj_kernel, rope=False, tn=tn),
        grid=(1, n_tiles),
        in_specs=[pl.BlockSpec((m, D_MODEL), lambda i, j: (0, 0)),
                  pl.BlockSpec((1, 1, 3 * D_MODEL), lambda i, j: (0, 0, 0)),
                  pl.BlockSpec((1, D_MODEL), lambda i, j: (0, 0)),
                  pl.BlockSpec((m, HEAD), lambda i, j: (0, 0)),
                  pl.BlockSpec((m, HEAD), lambda i, j: (0, 0)),
                  pl.BlockSpec((m, HEAD), lambda i, j: (0, 0)),
                  pl.BlockSpec((D_MODEL, tn), w_map)],
        out_specs=pl.BlockSpec((m, tn), lambda i, j: (0, j)),
        out_shape=jax.ShapeDtypeStruct((m, C_COLS), BF16),
        scratch_shapes=[pltpu.VMEM((m, D_MODEL), BF16)],
        compiler_params=pltpu.CompilerParams(
            dimension_semantics=("arbitrary", "arbitrary"), vmem_limit_bytes=VMEM_LIMIT),
        name="inproj_ctx",
    )(ctx2, modc3, gain, cos, slo, shi, w_bf)


def _attn_kernel(sink_ref, q_ref, ag_ref, kp_ref, kc_ref, kn_ref, vp_ref, vc_ref, vn_ref,
                 kx_ref, vx_ref, o_ref, *, n_blocks):
    i = pl.program_id(1)
    qb = q_ref.shape[0]
    n_ctx = kx_ref.shape[0]
    n_keys = 3 * qb + n_ctx
    row = lax.broadcasted_iota(jnp.int32, (ATT_GROUP * qb, n_keys), 0) & (qb - 1)
    col = lax.broadcasted_iota(jnp.int32, (ATT_GROUP * qb, n_keys), 1)
    off_prev = jnp.where(i > 0, 0, n_keys)
    off_next = jnp.where(i < n_blocks - 1, 0, n_keys)
    ok_prev = jnp.logical_or(col >= qb, col >= row + off_prev)
    ok_next = jnp.logical_or(jnp.logical_or(col < 2 * qb, col >= 3 * qb),
                             col - 2 * qb + off_next <= row)
    mask = jnp.logical_and(ok_prev, ok_next)
    for kh in range(ATT_KV_HEADS):
        ks = slice(kh * HEAD, (kh + 1) * HEAD)
        kcat = jnp.concatenate([kp_ref[:, ks], kc_ref[:, ks], kn_ref[:, ks], kx_ref[:, ks]], axis=0)
        vcat = jnp.concatenate([vp_ref[:, ks], vc_ref[:, ks], vn_ref[:, ks], vx_ref[:, ks]], axis=0)
        heads = [kh * ATT_GROUP + g for g in range(ATT_GROUP)]
        q4 = jnp.concatenate([q_ref[:, h * HEAD:(h + 1) * HEAD] for h in heads], axis=0)
        s = _dot_nt(q4, kcat)
        s = jnp.where(mask, s, -jnp.inf)
        sink = jnp.concatenate(
            [jnp.full((qb, 1), sink_ref[0, h], F32) for h in heads], axis=0)
        m = jnp.maximum(jnp.max(s, axis=-1, keepdims=True), sink)
        p = jnp.exp(s - m)
        den = jnp.sum(p, axis=-1, keepdims=True) + jnp.exp(sink - m)
        o4 = jnp.dot(p.astype(BF16), vcat, preferred_element_type=F32) / den
        for g, h in enumerate(heads):
            hs = slice(h * HEAD, (h + 1) * HEAD)
            gate = _silu(ag_ref[:, hs].astype(F32))
            o_ref[:, hs] = (o4[g * qb:(g + 1) * qb, :] * gate).astype(o_ref.dtype)


def _attention(p_lat, p_ctx, sink, batch, seq, n_ctx):
    qb = WINDOW
    nb = seq // qb
    kvw = ATT_KV_HEADS * HEAD
    kcol, vcol = P_K // kvw, P_V // kvw

    def rows(off):
        return lambda b, i: b * nb + jnp.clip(i + off, 0, nb - 1)

    def spec(w, rfn, c):
        return pl.BlockSpec((qb, w), lambda b, i: (rfn(b, i), c))

    return pl.pallas_call(
        functools.partial(_attn_kernel, n_blocks=nb),
        grid=(batch, nb),
        in_specs=[pl.BlockSpec(memory_space=pltpu.SMEM),
                  spec(D_MODEL, rows(0), P_Q // D_MODEL),
                  spec(D_MODEL, rows(0), P_AG // D_MODEL),
                  spec(kvw, rows(-1), kcol), spec(kvw, rows(0), kcol), spec(kvw, rows(1), kcol),
                  spec(kvw, rows(-1), vcol), spec(kvw, rows(0), vcol), spec(kvw, rows(1), vcol),
                  pl.BlockSpec((n_ctx, kvw), lambda b, i: (b, C_K // kvw)),
                  pl.BlockSpec((n_ctx, kvw), lambda b, i: (b, C_V // kvw))],
        out_specs=pl.BlockSpec((qb, D_MODEL), lambda b, i: (b * nb + i, 0)),
        out_shape=jax.ShapeDtypeStruct((batch * seq, D_MODEL), BF16),
        compiler_params=pltpu.CompilerParams(
            dimension_semantics=("arbitrary", "arbitrary"), vmem_limit_bytes=VMEM_LIMIT),
        name="attn",
    )(sink, p_lat, p_lat, p_lat, p_lat, p_lat, p_lat, p_lat, p_lat, p_ctx, p_ctx)


def _cumsum_rows(tri_bf, g):
    g1 = g.astype(BF16)
    r1 = g - g1.astype(F32)
    g2 = r1.astype(BF16)
    g3 = (r1 - g2.astype(F32)).astype(BF16)
    dot = functools.partial(jnp.dot, preferred_element_type=F32)
    return dot(tri_bf, g1) + dot(tri_bf, g2) + dot(tri_bf, g3)


def _hgrn_kernel(lbl_ref, cz_ref, cv_ref, q_ref, z_ref, v_ref, *rest, reverse, finalize,
                 heads, chunks_per_step):
    if finalize:
        hg_ref, ob_ref, gain_ref, o_ref, st_ref = rest
    else:
        o_ref, st_ref = rest
    t = pl.program_id(2)
    c = CHUNK
    row = lax.broadcasted_iota(jnp.int32, (c, c), 0)
    col = lax.broadcasted_iota(jnp.int32, (c, c), 1)
    tri = (col >= row) if reverse else (col <= row)
    tri_bf = jnp.where(tri, 1.0, 0.0).astype(BF16)
    end = 0 if reverse else c - 1
    mid = c // 2 if reverse else c // 2 - 1

    l0, l1 = lbl_ref[0:1, :], lbl_ref[1:2, :]
    lm = jnp.maximum(l0, l1)
    e0, e1 = jnp.exp(l0 - lm), jnp.exp(l1 - lm)
    lb = e0 / (e0 + e1)

    def gates(z):
        f = lb + (1.0 - lb) * jax.nn.sigmoid(z.astype(F32))
        g = jnp.log(f)
        return _cumsum_rows(tri_bf, g), 1.0 - f

    def state_update(h, gh, kh, vh):
        gend = gh[end:end + 1, :]
        ks = (kh * jnp.exp(gend - gh)).astype(BF16)
        st_ref[h] = jnp.exp(gend) * st_ref[h] + _dot_tn(vh, ks)

    @pl.when(t == 0)
    def _():
        st_ref[...] = jnp.zeros_like(st_ref)
        n_cc = cz_ref.shape[0] // c
        for ci in (range(n_cc - 1, -1, -1) if reverse else range(n_cc)):
            rs = slice(ci * c, (ci + 1) * c)
            gc, kc = gates(cz_ref[rs, :])
            for h in range(heads):
                hs = slice(h * HEAD, (h + 1) * HEAD)
                state_update(h, gc[:, hs], kc[:, hs], cv_ref[rs, hs])

    def chunk(ci, carry):
        ci = (chunks_per_step - 1 - ci) if reverse else ci
        rs = pl.ds(pl.multiple_of(ci * c, c), c)
        gc, kc = gates(z_ref[rs, :])
        for h in range(heads):
            hs = slice(h * HEAD, (h + 1) * HEAD)
            gh, kh = gc[:, hs], kc[:, hs]
            q = _silu(q_ref[rs, hs].astype(F32))
            vh = v_ref[rs, hs]
            beta = gh[mid:mid + 1, :]
            qa = (q * jnp.exp(gh - beta)).astype(BF16)
            ka = (kh * jnp.exp(beta - gh)).astype(BF16)
            a = jnp.where(tri, _dot_nt(qa, ka), 0.0)
            qi = (q * jnp.exp(gh)).astype(BF16)
            o = _dot_nt(qi, st_ref[h].astype(BF16)) + jnp.dot(
                a.astype(BF16), vh, preferred_element_type=F32)
            state_update(h, gh, kh, vh)
            if finalize:
                ot = o + ob_ref[rs, hs]
                y = ot * lax.rsqrt(jnp.mean(ot * ot, axis=-1, keepdims=True) + EPS)
                y = y * gain_ref[:, hs]
                o_ref[rs, hs] = (y * _silu(hg_ref[rs, hs].astype(F32))).astype(o_ref.dtype)
            else:
                o_ref[rs, hs] = o
        return carry

    lax.fori_loop(0, chunks_per_step, chunk, 0)


def _hgrn_pass(p_lat, p_ctx, lb_logits, o_back, gain, batch, seq, n_ctx, *, reverse):
    finalize = o_back is not None
    heads = 4
    hw = heads * HEAD
    cps = 4
    tb = cps * CHUNK
    steps = seq // tb
    hblocks = HG_HEADS // heads

    def rblk(b, t):
        return b * steps + ((steps - 1 - t) if reverse else t)

    def lat(col0):
        return pl.BlockSpec((tb, hw), lambda b, h, t: (rblk(b, t), col0 // hw + h))

    z_lat, z_ctx = (P_HFB, C_HFB) if reverse else (P_HFF, C_HFF)
    in_specs = [pl.BlockSpec((2, hw), lambda b, h, t: (0, h)),
                pl.BlockSpec((n_ctx, hw), lambda b, h, t: (b, z_ctx // hw + h)),
                pl.BlockSpec((n_ctx, hw), lambda b, h, t: (b, C_HI // hw + h)),
                lat(P_HQ), lat(z_lat), lat(P_HI)]
    args = [lb_logits, p_ctx, p_ctx, p_lat, p_lat, p_lat]
    if finalize:
        in_specs += [lat(P_HG), lat(0), pl.BlockSpec((1, hw), lambda b, h, t: (0, h))]
        args += [p_lat, o_back, gain]
    return pl.pallas_call(
        functools.partial(_hgrn_kernel, reverse=reverse, finalize=finalize, heads=heads,
                          chunks_per_step=cps),
        grid=(batch, hblocks, steps),
        in_specs=in_specs,
        out_specs=lat(0),
        out_shape=jax.ShapeDtypeStruct((batch * seq, HG_HEADS * HEAD), BF16 if finalize else F32),
        scratch_shapes=[pltpu.VMEM((heads, HEAD, HEAD), F32)],
        compiler_params=pltpu.CompilerParams(
            dimension_semantics=("arbitrary", "arbitrary", "arbitrary"),
            vmem_limit_bytes=VMEM_LIMIT),
        name="hgrn_fwd" if finalize else "hgrn_bwd",
    )(*args)


def _merge_kernel(hg_ref, at_ref, wh_ref, wa_ref, ma_ref, mb_ref, bm_ref, o_ref):
    ya = jnp.dot(hg_ref[...], wh_ref[...], preferred_element_type=F32)
    yb = jnp.dot(at_ref[...], wa_ref[...], preferred_element_type=F32)
    ga = jax.nn.sigmoid(ma_ref[...].astype(F32) + bm_ref[0:1, :])
    gb = jax.nn.sigmoid(mb_ref[...].astype(F32) + bm_ref[1:2, :])
    o_ref[...] = (ga * ya + gb * yb).astype(o_ref.dtype)


def _merge(hg_branch, att_branch, wh_bf, wa_bf, p_lat, b_merge):
    m = hg_branch.shape[0]
    tm, tn = 1024, 512
    return pl.pallas_call(
        _merge_kernel,
        grid=(m // tm, D_MODEL // tn),
        in_specs=[pl.BlockSpec((tm, D_MODEL), lambda i, j: (i, 0)),
                  pl.BlockSpec((tm, D_MODEL), lambda i, j: (i, 0)),
                  pl.BlockSpec((D_MODEL, tn), lambda i, j: (0, j)),
                  pl.BlockSpec((D_MODEL, tn), lambda i, j: (0, j)),
                  pl.BlockSpec((tm, tn), lambda i, j: (i, P_MA // tn + j)),
                  pl.BlockSpec((tm, tn), lambda i, j: (i, P_MB // tn + j)),
                  pl.BlockSpec((2, tn), lambda i, j: (0, j))],
        out_specs=pl.BlockSpec((tm, tn), lambda i, j: (i, j)),
        out_shape=jax.ShapeDtypeStruct((m, D_MODEL), BF16),
        compiler_params=pltpu.CompilerParams(
            dimension_semantics=("arbitrary", "arbitrary"), vmem_limit_bytes=VMEM_LIMIT),
        name="merge",
    )(hg_branch, att_branch, wh_bf, wa_bf, p_lat, p_lat, b_merge)


def _out_kernel(y_ref, w_ref, x_ref, mod_ref, gain_ref, o_ref):
    z = jnp.dot(y_ref[...], w_ref[...], preferred_element_type=F32)
    gate = mod_ref[0, :, 2 * D_MODEL:3 * D_MODEL]
    h = x_ref[...] + gate * z
    y = h * lax.rsqrt(jnp.mean(h * h, axis=-1, keepdims=True) + EPS)
    o_ref[...] = y * gain_ref[...]


def _out(y, wo_bf, x2, mod3, fgain, seq):
    m = y.shape[0]
    tm = 256
    tiles_per_seq = seq // tm
    return pl.pallas_call(
        _out_kernel,
        grid=(m // tm,),
        in_specs=[pl.BlockSpec((tm, D_MODEL), lambda i: (i, 0)),
                  pl.BlockSpec((D_MODEL, D_MODEL), lambda i: (0, 0)),
                  pl.BlockSpec((tm, D_MODEL), lambda i: (i, 0)),
                  pl.BlockSpec((1, 1, 3 * D_MODEL), lambda i: (i // tiles_per_seq, 0, 0)),
                  pl.BlockSpec((1, D_MODEL), lambda i: (0, 0))],
        out_specs=pl.BlockSpec((tm, D_MODEL), lambda i: (i, 0)),
        out_shape=jax.ShapeDtypeStruct((m, D_MODEL), F32),
        compiler_params=pltpu.CompilerParams(
            dimension_semantics=("arbitrary",), vmem_limit_bytes=VMEM_LIMIT),
        name="out",
    )(y, wo_bf, x2, mod3, fgain)


def _rope_tables(seq):
    rows = seq // GRID_W
    row = jnp.repeat(jnp.arange(rows, dtype=F32), GRID_W)
    col = jnp.tile(jnp.arange(GRID_W, dtype=F32), rows)
    half = HEAD // 2
    inv_freq = 1.0 / (ROPE_BASE ** (jnp.arange(0, half, 2, dtype=F32) / half))
    ang_r = row[:, None] * inv_freq[None, :]
    ang_c = col[:, None] * inv_freq[None, :]
    ang = jnp.concatenate([ang_r, ang_r, ang_c, ang_c], axis=-1)
    cos, sin = jnp.cos(ang), jnp.sin(ang)
    lo = (jnp.arange(HEAD) % half) < (half // 2)
    return cos, jnp.where(lo, -sin, 0.0), jnp.where(lo, 0.0, sin)


def kernel(x, c, ctx, c_ctx, w_ada, b_ada, norm_gain, w_in, b_merge, lb_logits_fwd,
           lb_logits_bwd, hgrn_norm_gain, w_o_hgrn, sink_logits, w_o_attn, w_out,
           final_norm_gain):
    batch, seq, d = x.shape
    n_ctx = ctx.shape[1]
    assert d == D_MODEL and w_ada.shape[0] == 1 and seq % 1024 == 0 and batch * n_ctx == 1024

    c8 = jnp.concatenate([c, c_ctx[None, :], jnp.zeros((8 - batch - 1, d), F32)], axis=0)
    mod8 = _ada(c8, w_ada[0], b_ada[0][None, :])
    mod3 = mod8[:batch, None, :]
    modc3 = mod8[batch:batch + 1, None, :]

    cos, slo, shi = _rope_tables(seq)
    w_bf = w_in[0].astype(BF16)
    gain = norm_gain[0][None, :]
    x2 = x.reshape(batch * seq, d)
    p_lat = _inproj_latent(x2, mod3, gain, cos, slo, shi, w_bf, seq)
    p_ctx = _inproj_ctx(ctx.reshape(batch * n_ctx, d), modc3, gain,
                        cos[:batch * n_ctx], slo[:batch * n_ctx], shi[:batch * n_ctx], w_bf)

    att_branch = _attention(p_lat, p_ctx, sink_logits[0][None, :], batch, seq, n_ctx)

    o_back = _hgrn_pass(p_lat, p_ctx, lb_logits_bwd, None, None, batch, seq, n_ctx, reverse=True)
    hg_branch = _hgrn_pass(p_lat, p_ctx, lb_logits_fwd, o_back, hgrn_norm_gain[0][None, :],
                           batch, seq, n_ctx, reverse=False)

    y = _merge(hg_branch, att_branch, w_o_hgrn[0].astype(BF16), w_o_attn[0].astype(BF16),
               p_lat, b_merge[0])
    out = _out(y, w_out[0].astype(BF16), x2, mod3, final_norm_gain[None, :], seq)
    return out.reshape(batch, seq, d)
```

```python
import functools

import jax
import jax.numpy as jnp
import numpy as np
from jax import lax
from jax.experimental import pallas as pl
from jax.experimental.pallas import tpu as pltpu

F32 = jnp.float32
BF16 = jnp.bfloat16

D_MODEL = 2048
GRID_W = 64
EPS = 1e-6
HEAD = 128
ATT_HEADS = 16
ATT_KV_HEADS = 4
ATT_GROUP = ATT_HEADS // ATT_KV_HEADS
WINDOW = 128
ROPE_BASE = 10000.0
HG_HEADS = 16
CHUNK = 64
IN_COLS = 19456

P_Q, P_AG, P_HQ, P_HFF, P_HFB, P_HI, P_HG, P_MA, P_MB, P_K, P_V = (
    0, 2048, 4096, 6144, 8192, 10240, 12288, 14336, 16384, 18432, 18944)
C_K, C_V, C_HFF, C_HFB, C_HI, C_COLS = 0, 512, 1024, 3072, 5120, 7168

VMEM_LIMIT = 56 * 1024 * 1024


def _silu(x):
    return x * jax.nn.sigmoid(x)


def _dot_nt(a, b):
    return lax.dot_general(a, b, (((1,), (1,)), ((), ())), preferred_element_type=F32)


def _dot_tn(a, b):
    return lax.dot_general(a, b, (((0,), (0,)), ((), ())), preferred_element_type=F32)


def _ada_kernel(c_ref, w_ref, b_ref, o_ref):
    s = _silu(c_ref[...])
    o_ref[...] = jnp.dot(s.astype(BF16), w_ref[...].astype(BF16),
                         preferred_element_type=F32) + b_ref[...]


def _ada(c8, w_ada, b_ada):
    n_out = w_ada.shape[1]
    tn = 1024
    return pl.pallas_call(
        _ada_kernel,
        grid=(n_out // tn,),
        in_specs=[pl.BlockSpec((8, D_MODEL), lambda j: (0, 0)),
                  pl.BlockSpec((D_MODEL, tn), lambda j: (0, j)),
                  pl.BlockSpec((1, tn), lambda j: (0, j))],
        out_specs=pl.BlockSpec((8, tn), lambda j: (0, j)),
        out_shape=jax.ShapeDtypeStruct((8, n_out), F32),
        compiler_params=pltpu.CompilerParams(
            dimension_semantics=("arbitrary",), vmem_limit_bytes=VMEM_LIMIT),
        name="ada",
    )(c8, w_ada, b_ada)


def _rope(t, cos, sin_lo, sin_hi):
    return t * cos + pltpu.roll(t, 96, 1) * sin_lo + pltpu.roll(t, 32, 1) * sin_hi


def _inproj_kernel(x_ref, mod_ref, gain_ref, cos_ref, slo_ref, shi_ref, w_ref, o_ref, xn_ref,
                   *, rope, tn):
    j = pl.program_id(1)

    @pl.when(j == 0)
    def _():
        x = x_ref[...]
        y = x * lax.rsqrt(jnp.mean(x * x, axis=-1, keepdims=True) + EPS) * gain_ref[...]
        shift = mod_ref[0, :, 0:D_MODEL]
        scale = mod_ref[0, :, D_MODEL:2 * D_MODEL]
        xn_ref[...] = (y * (1.0 + scale) + shift).astype(BF16)

    acc = jnp.dot(xn_ref[...], w_ref[...], preferred_element_type=F32)
    if not rope:
        o_ref[...] = acc.astype(o_ref.dtype)
        return

    n_heads = tn // HEAD
    q_tiles = P_AG // tn
    kv_tile = P_K // tn
    k_heads = (P_V - P_K) // HEAD
    q_scale = HEAD ** -0.5

    def store_rope(h, mult):
        sl = slice(h * HEAD, (h + 1) * HEAD)
        r = _rope(acc[:, sl], cos_ref[...], slo_ref[...], shi_ref[...])
        if mult is not None:
            r = r * mult
        o_ref[:, sl] = r.astype(o_ref.dtype)

    @pl.when(j < q_tiles)
    def _():
        for h in range(n_heads):
            store_rope(h, q_scale)

    @pl.when(j == kv_tile)
    def _():
        for h in range(k_heads):
            store_rope(h, None)
        o_ref[:, k_heads * HEAD:] = acc[:, k_heads * HEAD:].astype(o_ref.dtype)

    @pl.when(jnp.logical_and(j >= q_tiles, j != kv_tile))
    def _():
        o_ref[...] = acc.astype(o_ref.dtype)


def _inproj_latent(x2, mod3, gain, cos, slo, shi, w_bf, seq):
    m = x2.shape[0]
    tm, tn = 1024, 1024
    tiles_per_seq = seq // tm
    n_tiles = IN_COLS // tn

    def w_map(i, j):
        src = jnp.where(j < 2, j, jnp.where(j == n_tiles - 1, 2, j + 1))
        return (0, src)

    return pl.pallas_call(
        functools.partial(_inproj_kernel, rope=True, tn=tn),
        grid=(m // tm, n_tiles),
        in_specs=[pl.BlockSpec((tm, D_MODEL), lambda i, j: (i, 0)),
                  pl.BlockSpec((1, 1, 3 * D_MODEL), lambda i, j: (i // tiles_per_seq, 0, 0)),
                  pl.BlockSpec((1, D_MODEL), lambda i, j: (0, 0)),
                  pl.BlockSpec((tm, HEAD), lambda i, j: (i % tiles_per_seq, 0)),
                  pl.BlockSpec((tm, HEAD), lambda i, j: (i % tiles_per_seq, 0)),
                  pl.BlockSpec((tm, HEAD), lambda i, j: (i % tiles_per_seq, 0)),
                  pl.BlockSpec((D_MODEL, tn), w_map)],
        out_specs=pl.BlockSpec((tm, tn), lambda i, j: (i, j)),
        out_shape=jax.ShapeDtypeStruct((m, IN_COLS), BF16),
        scratch_shapes=[pltpu.VMEM((tm, D_MODEL), BF16)],
        compiler_params=pltpu.CompilerParams(
            dimension_semantics=("arbitrary", "arbitrary"), vmem_limit_bytes=VMEM_LIMIT),
        name="inproj",
    )(x2, mod3, gain, cos, slo, shi, w_bf)


def _inproj_ctx(ctx2, modc3, gain, cos, slo, shi, w_bf):
    m = ctx2.shape[0]
    tn = 1024
    n_tiles = C_COLS // tn

    def w_map(i, j):
        return (0, jnp.where(j == 0, 2, j + 6))

    return pl.pallas_call(
        functools.partial(_inproj_kernel, rope=False, tn=tn),
        grid=(1, n_tiles),
        in_specs=[pl.BlockSpec((m, D_MODEL), lambda i, j: (0, 0)),
                  pl.BlockSpec((1, 1, 3 * D_MODEL), lambda i, j: (0, 0, 0)),
                  pl.BlockSpec((1, D_MODEL), lambda i, j: (0, 0)),
                  pl.BlockSpec((m, HEAD), lambda i, j: (0, 0)),
                  pl.BlockSpec((m, HEAD), lambda i, j: (0, 0)),
                  pl.BlockSpec((m, HEAD), lambda i, j: (0, 0)),
                  pl.BlockSpec((D_MODEL, tn), w_map)],
        out_specs=pl.BlockSpec((m, tn), lambda i, j: (0, j)),
        out_shape=jax.ShapeDtypeStruct((m, C_COLS), BF16),
        scratch_shapes=[pltpu.VMEM((m, D_MODEL), BF16)],
        compiler_params=pltpu.CompilerParams(
            dimension_semantics=("arbitrary", "arbitrary"), vmem_limit_bytes=VMEM_LIMIT),
        name="inproj_ctx",
    )(ctx2, modc3, gain, cos, slo, shi, w_bf)


def _attn_kernel(sink_ref, q_ref, ag_ref, kp_ref, kc_ref, kn_ref, vp_ref, vc_ref, vn_ref,
                 kx_ref, vx_ref, o_ref, *, n_blocks):
    i = pl.program_id(1)
    qb = q_ref.shape[0]
    n_ctx = kx_ref.shape[0]
    n_keys = 3 * qb + n_ctx
    row = lax.broadcasted_iota(jnp.int32, (ATT_GROUP * qb, n_keys), 0) & (qb - 1)
    col = lax.broadcasted_iota(jnp.int32, (ATT_GROUP * qb, n_keys), 1)
    off_prev = jnp.where(i > 0, 0, n_keys)
    off_next = jnp.where(i < n_blocks - 1, 0, n_keys)
    ok_prev = jnp.logical_or(col >= qb, col >= row + off_prev)
    ok_next = jnp.logical_or(jnp.logical_or(col < 2 * qb, col >= 3 * qb),
                             col - 2 * qb + off_next <= row)
    mask = jnp.logical_and(ok_prev, ok_next)
    for kh in range(ATT_KV_HEADS):
        ks = slice(kh * HEAD, (kh + 1) * HEAD)
        kcat = jnp.concatenate([kp_ref[:, ks], kc_ref[:, ks], kn_ref[:, ks], kx_ref[:, ks]], axis=0)
        vcat = jnp.concatenate([vp_ref[:, ks], vc_ref[:, ks], vn_ref[:, ks], vx_ref[:, ks]], axis=0)
        heads = [kh * ATT_GROUP + g for g in range(ATT_GROUP)]
        q4 = jnp.concatenate([q_ref[:, h * HEAD:(h + 1) * HEAD] for h in heads], axis=0)
        s = _dot_nt(q4, kcat)
        s = jnp.where(mask, s, -jnp.inf)
        sink = jnp.concatenate(
            [jnp.full((qb, 1), sink_ref[0, h], F32) for h in heads], axis=0)
        m = jnp.maximum(jnp.max(s, axis=-1, keepdims=True), sink)
        p = jnp.exp(s - m)
        den = jnp.sum(p, axis=-1, keepdims=True) + jnp.exp(sink - m)
        o4 = jnp.dot(p.astype(BF16), vcat, preferred_element_type=F32) / den
        for g, h in enumerate(heads):
            hs = slice(h * HEAD, (h + 1) * HEAD)
            gate = _silu(ag_ref[:, hs].astype(F32))
            o_ref[:, hs] = (o4[g * qb:(g + 1) * qb, :] * gate).astype(o_ref.dtype)


def _attention(p_lat, p_ctx, sink, batch, seq, n_ctx):
    qb = WINDOW
    nb = seq // qb
    kvw = ATT_KV_HEADS * HEAD
    kcol, vcol = P_K // kvw, P_V // kvw

    def rows(off):
        return lambda b, i: b * nb + jnp.clip(i + off, 0, nb - 1)

    def spec(w, rfn, c):
        return pl.BlockSpec((qb, w), lambda b, i: (rfn(b, i), c))

    return pl.pallas_call(
        functools.partial(_attn_kernel, n_blocks=nb),
        grid=(batch, nb),
        in_specs=[pl.BlockSpec(memory_space=pltpu.SMEM),
                  spec(D_MODEL, rows(0), P_Q // D_MODEL),
                  spec(D_MODEL, rows(0), P_AG // D_MODEL),
                  spec(kvw, rows(-1), kcol), spec(kvw, rows(0), kcol), spec(kvw, rows(1), kcol),
                  spec(kvw, rows(-1), vcol), spec(kvw, rows(0), vcol), spec(kvw, rows(1), vcol),
                  pl.BlockSpec((n_ctx, kvw), lambda b, i: (b, C_K // kvw)),
                  pl.BlockSpec((n_ctx, kvw), lambda b, i: (b, C_V // kvw))],
        out_specs=pl.BlockSpec((qb, D_MODEL), lambda b, i: (b * nb + i, 0)),
        out_shape=jax.ShapeDtypeStruct((batch * seq, D_MODEL), BF16),
        compiler_params=pltpu.CompilerParams(
            dimension_semantics=("arbitrary", "arbitrary"), vmem_limit_bytes=VMEM_LIMIT),
        name="attn",
    )(sink, p_lat, p_lat, p_lat, p_lat, p_lat, p_lat, p_lat, p_lat, p_ctx, p_ctx)


SUB = 16
N_SUB = CHUNK // SUB
ROW_FACTORS = 24


def _hgrn_decay_matrix(reverse):
    m = np.zeros((2 * CHUNK + 32, CHUNK), np.float32)
    r = 2 * CHUNK
    for i in range(N_SUB):
        lo, hi, mid = i * SUB, (i + 1) * SUB, i * SUB + SUB // 2
        for t in range(lo, hi):
            if reverse:
                m[t, t:hi] = 1
                m[CHUNK + t, lo:t] = 1
            else:
                m[t, lo:t + 1] = 1
                m[CHUNK + t, t + 1:hi] = 1
        if reverse:
            before, after, first, second = slice(hi, CHUNK), slice(0, lo), slice(mid, hi), slice(lo, mid)
        else:
            before, after, first, second = slice(0, lo), slice(hi, CHUNK), slice(lo, mid), slice(mid, hi)
        m[r + i, before] = 1
        m[r + 4 + i, after] = 1
        m[r + 8 + i, first] = -1
        m[r + 12 + i, second] = -1
    b2, b1 = (1, 2) if reverse else (2, 1)
    m[r + 16, b2 * SUB:(b2 + 1) * SUB] = 1
    m[r + 17, b1 * SUB:(b1 + 1) * SUB] = 1
    m[r + 18, :] = 1
    return np.concatenate([m, m], axis=1)


def _cumsum_rows(tri_bf, g):
    g1 = g.astype(BF16)
    r1 = g - g1.astype(F32)
    g2 = r1.astype(BF16)
    g3 = (r1 - g2.astype(F32)).astype(BF16)
    dot = functools.partial(jnp.dot, preferred_element_type=F32)
    return dot(tri_bf, g1) + dot(tri_bf, g2) + dot(tri_bf, g3)


def _hgrn_kernel(lbl_ref, dm_ref, cz_ref, cv_ref, q_ref, z_ref, v_ref, *rest, reverse, finalize,
                 heads, chunks_per_step):
    if finalize:
        hg_ref, ob_ref, gain_ref, o_ref, st_ref = rest
    else:
        o_ref, st_ref = rest
    t = pl.program_id(2)
    c = CHUNK
    row = lax.broadcasted_iota(jnp.int32, (c, c), 0)
    col = lax.broadcasted_iota(jnp.int32, (c, c), 1)
    tri = (col >= row) if reverse else (col <= row)
    end = 0 if reverse else c - 1

    l0, l1 = lbl_ref[0:1, :], lbl_ref[1:2, :]
    lm = jnp.maximum(l0, l1)
    e0, e1 = jnp.exp(l0 - lm), jnp.exp(l1 - lm)
    lb = e0 / (e0 + e1)
    fa, fb = lb + 0.5 * (1.0 - lb), 0.5 * (1.0 - lb)

    def forget(z):
        f = fa + fb * jnp.tanh(0.5 * z.astype(F32))
        return jnp.log(f), 1.0 - f

    @pl.when(t == 0)
    def _():
        tri_bf = jnp.where(tri, 1.0, 0.0).astype(BF16)
        n_cc = cz_ref.shape[0] // c
        s_ctx = [jnp.zeros((HEAD, HEAD), F32) for _ in range(heads)]
        for ci in (range(n_cc - 1, -1, -1) if reverse else range(n_cc)):
            rs = slice(ci * c, (ci + 1) * c)
            g, kk = forget(cz_ref[rs, :])
            gc = _cumsum_rows(tri_bf, g)
            for h in range(heads):
                hs = slice(h * HEAD, (h + 1) * HEAD)
                gend = gc[end:end + 1, hs]
                ks = (kk[:, hs] * jnp.exp(gend - gc[:, hs])).astype(BF16)
                s_ctx[h] = jnp.exp(gend) * s_ctx[h] + _dot_tn(cv_ref[rs, hs], ks)
        for h in range(heads):
            st_ref[h] = s_ctx[h]

    def blk(x, i):
        return x[i * SUB:(i + 1) * SUB]

    def tb(tau):
        return N_SUB - 1 - tau if reverse else tau

    zero = jnp.zeros((SUB, HEAD), BF16)

    def place(parts):
        return jnp.concatenate([parts.get(i, zero) for i in range(N_SUB)], axis=0)

    diag = jnp.logical_and(tri, (row ^ col) < SUB)

    def scores(ci):
        rs = slice(ci * c, (ci + 1) * c)
        g, kk = forget(z_ref[rs, :])
        g1 = g.astype(BF16)
        g2 = (g - g1.astype(F32)).astype(BF16)
        ld = jnp.dot(dm_ref[...], jnp.concatenate([g1, g2], axis=0), preferred_element_type=F32)
        hq = 0.5 * q_ref[rs, :].astype(F32)
        q_hat = (hq + hq * jnp.tanh(hq)) * jnp.exp(ld[0:c])
        k_hat = kk * jnp.exp(ld[c:2 * c])
        rf = jnp.exp(ld[2 * c:2 * c + ROW_FACTORS])
        per_head = []
        for h in range(heads):
            hs = slice(h * HEAD, (h + 1) * HEAD)
            q_h, k_h = q_hat[:, hs], k_hat[:, hs]

            def scaled(x, r0):
                return jnp.concatenate(
                    [blk(x, i) * rf[r0 + i:r0 + i + 1, hs] for i in range(N_SUB)],
                    axis=0).astype(BF16)

            qi, ks, qd, kd = scaled(q_h, 0), scaled(k_h, 4), scaled(q_h, 8), scaled(k_h, 12)
            qb, kb = q_h.astype(BF16), k_h.astype(BF16)
            q3 = (blk(q_h, tb(3)) * rf[16:17, hs]).astype(BF16)
            k0 = (blk(k_h, tb(0)) * rf[17:18, hs]).astype(BF16)
            ql = jnp.concatenate(
                [place({tb(1): blk(qb, tb(1))}), place({tb(3): blk(qb, tb(3))}),
                 place({tb(2): blk(qb, tb(2)), tb(3): q3})], axis=1)
            kl = jnp.concatenate(
                [place({tb(0): blk(kb, tb(0))}), place({tb(2): blk(kb, tb(2))}),
                 place({tb(1): blk(kb, tb(1)), tb(0): k0})], axis=1)
            per_head.append((qi, ks, _dot_nt(ql, kl), _dot_nt(qd, kd), rf[18:19, hs]))
        return per_head

    def outputs(ci, per_head, state):
        rs = slice(ci * c, (ci + 1) * c)
        a_bf = [jnp.where(diag, a_dg, a_off).astype(BF16) for _, _, a_off, a_dg, _ in per_head]
        outs, new_state = [], []
        for h in range(heads):
            hs = slice(h * HEAD, (h + 1) * HEAD)
            qi, ks, _, _, dec = per_head[h]
            v_h = v_ref[rs, hs]
            outs.append(_dot_nt(qi, state[h].astype(BF16))
                        + jnp.dot(a_bf[h], v_h, preferred_element_type=F32))
            new_state.append(dec * state[h] + _dot_tn(v_h, ks))
        for h in range(heads):
            hs = slice(h * HEAD, (h + 1) * HEAD)
            if finalize:
                ot = outs[h] + ob_ref[rs, hs]
                y = ot * lax.rsqrt(jnp.mean(ot * ot, axis=-1, keepdims=True) + EPS)
                y = y * gain_ref[:, hs]
                o_ref[rs, hs] = (y * _silu(hg_ref[rs, hs].astype(F32))).astype(o_ref.dtype)
            else:
                o_ref[rs, hs] = outs[h]
        return new_state

    state = [st_ref[h] for h in range(heads)]
    order = list(range(chunks_per_step - 1, -1, -1) if reverse else range(chunks_per_step))
    pending = scores(order[0])
    for n, ci in enumerate(order):
        upcoming = scores(order[n + 1]) if n + 1 < len(order) else None
        state = outputs(ci, pending, state)
        pending = upcoming
    for h in range(heads):
        st_ref[h] = state[h]


def _hgrn_pass(p_lat, p_ctx, lb_logits, o_back, gain, batch, seq, n_ctx, *, reverse):
    finalize = o_back is not None
    heads = 8
    hw = heads * HEAD
    cps = 4
    tb = cps * CHUNK
    steps = seq // tb
    hblocks = HG_HEADS // heads
    dm = jnp.asarray(_hgrn_decay_matrix(reverse), BF16)

    def rblk(b, t):
        return b * steps + ((steps - 1 - t) if reverse else t)

    def lat(col0):
        return pl.BlockSpec((tb, hw), lambda b, h, t: (rblk(b, t), col0 // hw + h))

    z_lat, z_ctx = (P_HFB, C_HFB) if reverse else (P_HFF, C_HFF)
    in_specs = [pl.BlockSpec((2, hw), lambda b, h, t: (0, h)),
                pl.BlockSpec(dm.shape, lambda b, h, t: (0, 0)),
                pl.BlockSpec((n_ctx, hw), lambda b, h, t: (b, z_ctx // hw + h)),
                pl.BlockSpec((n_ctx, hw), lambda b, h, t: (b, C_HI // hw + h)),
                lat(P_HQ), lat(z_lat), lat(P_HI)]
    args = [lb_logits, dm, p_ctx, p_ctx, p_lat, p_lat, p_lat]
    if finalize:
        in_specs += [lat(P_HG), lat(0), pl.BlockSpec((1, hw), lambda b, h, t: (0, h))]
        args += [p_lat, o_back, gain]
    return pl.pallas_call(
        functools.partial(_hgrn_kernel, reverse=reverse, finalize=finalize, heads=heads,
                          chunks_per_step=cps),
        grid=(batch, hblocks, steps),
        in_specs=in_specs,
        out_specs=lat(0),
        out_shape=jax.ShapeDtypeStruct((batch * seq, HG_HEADS * HEAD), BF16 if finalize else F32),
        scratch_shapes=[pltpu.VMEM((heads, HEAD, HEAD), F32)],
        compiler_params=pltpu.CompilerParams(
            dimension_semantics=("arbitrary", "arbitrary", "arbitrary"),
            vmem_limit_bytes=VMEM_LIMIT),
        name="hgrn_fwd" if finalize else "hgrn_bwd",
    )(*args)


def _merge_kernel(hg_ref, at_ref, wh_ref, wa_ref, ma_ref, mb_ref, bm_ref, o_ref):
    ya = jnp.dot(hg_ref[...], wh_ref[...], preferred_element_type=F32)
    yb = jnp.dot(at_ref[...], wa_ref[...], preferred_element_type=F32)
    ga = jax.nn.sigmoid(ma_ref[...].astype(F32) + bm_ref[0:1, :])
    gb = jax.nn.sigmoid(mb_ref[...].astype(F32) + bm_ref[1:2, :])
    o_ref[...] = (ga * ya + gb * yb).astype(o_ref.dtype)


def _merge(hg_branch, att_branch, wh_bf, wa_bf, p_lat, b_merge):
    m = hg_branch.shape[0]
    tm, tn = 1024, 512
    return pl.pallas_call(
        _merge_kernel,
        grid=(m // tm, D_MODEL // tn),
        in_specs=[pl.BlockSpec((tm, D_MODEL), lambda i, j: (i, 0)),
                  pl.BlockSpec((tm, D_MODEL), lambda i, j: (i, 0)),
                  pl.BlockSpec((D_MODEL, tn), lambda i, j: (0, j)),
                  pl.BlockSpec((D_MODEL, tn), lambda i, j: (0, j)),
                  pl.BlockSpec((tm, tn), lambda i, j: (i, P_MA // tn + j)),
                  pl.BlockSpec((tm, tn), lambda i, j: (i, P_MB // tn + j)),
                  pl.BlockSpec((2, tn), lambda i, j: (0, j))],
        out_specs=pl.BlockSpec((tm, tn), lambda i, j: (i, j)),
        out_shape=jax.ShapeDtypeStruct((m, D_MODEL), BF16),
        compiler_params=pltpu.CompilerParams(
            dimension_semantics=("arbitrary", "arbitrary"), vmem_limit_bytes=VMEM_LIMIT),
        name="merge",
    )(hg_branch, att_branch, wh_bf, wa_bf, p_lat, p_lat, b_merge)


def _out_kernel(y_ref, w_ref, x_ref, mod_ref, gain_ref, o_ref):
    z = jnp.dot(y_ref[...], w_ref[...], preferred_element_type=F32)
    gate = mod_ref[0, :, 2 * D_MODEL:3 * D_MODEL]
    h = x_ref[...] + gate * z
    y = h * lax.rsqrt(jnp.mean(h * h, axis=-1, keepdims=True) + EPS)
    o_ref[...] = y * gain_ref[...]


def _out(y, wo_bf, x2, mod3, fgain, seq):
    m = y.shape[0]
    tm = 256
    tiles_per_seq = seq // tm
    return pl.pallas_call(
        _out_kernel,
        grid=(m // tm,),
        in_specs=[pl.BlockSpec((tm, D_MODEL), lambda i: (i, 0)),
                  pl.BlockSpec((D_MODEL, D_MODEL), lambda i: (0, 0)),
                  pl.BlockSpec((tm, D_MODEL), lambda i: (i, 0)),
                  pl.BlockSpec((1, 1, 3 * D_MODEL), lambda i: (i // tiles_per_seq, 0, 0)),
                  pl.BlockSpec((1, D_MODEL), lambda i: (0, 0))],
        out_specs=pl.BlockSpec((tm, D_MODEL), lambda i: (i, 0)),
        out_shape=jax.ShapeDtypeStruct((m, D_MODEL), F32),
        compiler_params=pltpu.CompilerParams(
            dimension_semantics=("arbitrary",), vmem_limit_bytes=VMEM_LIMIT),
        name="out",
    )(y, wo_bf, x2, mod3, fgain)


def _rope_tables(seq):
    rows = seq // GRID_W
    row = jnp.repeat(jnp.arange(rows, dtype=F32), GRID_W)
    col = jnp.tile(jnp.arange(GRID_W, dtype=F32), rows)
    half = HEAD // 2
    inv_freq = 1.0 / (ROPE_BASE ** (jnp.arange(0, half, 2, dtype=F32) / half))
    ang_r = row[:, None] * inv_freq[None, :]
    ang_c = col[:, None] * inv_freq[None, :]
    ang = jnp.concatenate([ang_r, ang_r, ang_c, ang_c], axis=-1)
    cos, sin = jnp.cos(ang), jnp.sin(ang)
    lo = (jnp.arange(HEAD) % half) < (half // 2)
    return cos, jnp.where(lo, -sin, 0.0), jnp.where(lo, 0.0, sin)


def kernel(x, c, ctx, c_ctx, w_ada, b_ada, norm_gain, w_in, b_merge, lb_logits_fwd,
           lb_logits_bwd, hgrn_norm_gain, w_o_hgrn, sink_logits, w_o_attn, w_out,
           final_norm_gain):
    batch, seq, d = x.shape
    n_ctx = ctx.shape[1]
    assert d == D_MODEL and w_ada.shape[0] == 1 and seq % 1024 == 0 and batch * n_ctx == 1024

    c8 = jnp.concatenate([c, c_ctx[None, :], jnp.zeros((8 - batch - 1, d), F32)], axis=0)
    mod8 = _ada(c8, w_ada[0], b_ada[0][None, :])
    mod3 = mod8[:batch, None, :]
    modc3 = mod8[batch:batch + 1, None, :]

    cos, slo, shi = _rope_tables(seq)
    w_bf = w_in[0].astype(BF16)
    gain = norm_gain[0][None, :]
    x2 = x.reshape(batch * seq, d)
    p_lat = _inproj_latent(x2, mod3, gain, cos, slo, shi, w_bf, seq)
    p_ctx = _inproj_ctx(ctx.reshape(batch * n_ctx, d), modc3, gain,
                        cos[:batch * n_ctx], slo[:batch * n_ctx], shi[:batch * n_ctx], w_bf)

    att_branch = _attention(p_lat, p_ctx, sink_logits[0][None, :], batch, seq, n_ctx)

    o_back = _hgrn_pass(p_lat, p_ctx, lb_logits_bwd, None, None, batch, seq, n_ctx, reverse=True)
    hg_branch = _hgrn_pass(p_lat, p_ctx, lb_logits_fwd, o_back, hgrn_norm_gain[0][None, :],
                           batch, seq, n_ctx, reverse=False)

    y = _merge(hg_branch, att_branch, w_o_hgrn[0].astype(BF16), w_o_attn[0].astype(BF16),
               p_lat, b_merge[0])
    out = _out(y, w_out[0].astype(BF16), x2, mod3, final_norm_gain[None, :], seq)
    return out.reshape(batch, seq, d)
```

```python
import functools

import jax
import jax.numpy as jnp
import numpy as np
from jax import lax
from jax.experimental import pallas as pl
from jax.experimental.pallas import tpu as pltpu

F32 = jnp.float32
BF16 = jnp.bfloat16

D_MODEL = 2048
GRID_W = 64
EPS = 1e-6
HEAD = 128
ATT_HEADS = 16
ATT_KV_HEADS = 4
ATT_GROUP = ATT_HEADS // ATT_KV_HEADS
WINDOW = 128
ROPE_BASE = 10000.0
LOG2_E = 1.4426950408889634
HG_HEADS = 16
CHUNK = 64
IN_COLS = 19456

P_Q, P_AG, P_HQ, P_HFF, P_HFB, P_HI, P_HG, P_MA, P_MB, P_K, P_V = (
    0, 2048, 4096, 6144, 8192, 10240, 12288, 14336, 16384, 18432, 18944)
C_K, C_V, C_HFF, C_HFB, C_HI, C_COLS = 0, 512, 1024, 3072, 5120, 7168

VMEM_LIMIT = 56 * 1024 * 1024


def _silu(x):
    h = 0.5 * x
    return h + h * jnp.tanh(h)


def _dot_nt(a, b):
    return lax.dot_general(a, b, (((1,), (1,)), ((), ())), preferred_element_type=F32)


def _dot_tn(a, b):
    return lax.dot_general(a, b, (((0,), (0,)), ((), ())), preferred_element_type=F32)


def _ada_kernel(c_ref, w_ref, b_ref, o_ref):
    s = _silu(c_ref[...])
    o_ref[...] = jnp.dot(s.astype(BF16), w_ref[...].astype(BF16),
                         preferred_element_type=F32) + b_ref[...]


def _ada(c8, w_ada, b_ada):
    n_out = w_ada.shape[1]
    tn = 1024
    return pl.pallas_call(
        _ada_kernel,
        grid=(n_out // tn,),
        in_specs=[pl.BlockSpec((8, D_MODEL), lambda j: (0, 0)),
                  pl.BlockSpec((D_MODEL, tn), lambda j: (0, j)),
                  pl.BlockSpec((1, tn), lambda j: (0, j))],
        out_specs=pl.BlockSpec((8, tn), lambda j: (0, j)),
        out_shape=jax.ShapeDtypeStruct((8, n_out), F32),
        compiler_params=pltpu.CompilerParams(
            dimension_semantics=("arbitrary",), vmem_limit_bytes=VMEM_LIMIT),
        name="ada",
    )(c8, w_ada, b_ada)


def _rope(t, cos, sin_lo, sin_hi):
    return t * cos + pltpu.roll(t, 96, 1) * sin_lo + pltpu.roll(t, 32, 1) * sin_hi


def _inproj_kernel(x_ref, mod_ref, gain_ref, cos_ref, slo_ref, shi_ref, w_ref, o_ref, xn_ref,
                   *, rope, tn):
    j = pl.program_id(1)

    @pl.when(j == 0)
    def _():
        x = x_ref[...]
        y = x * lax.rsqrt(jnp.mean(x * x, axis=-1, keepdims=True) + EPS) * gain_ref[...]
        shift = mod_ref[0, :, 0:D_MODEL]
        scale = mod_ref[0, :, D_MODEL:2 * D_MODEL]
        xn_ref[...] = (y * (1.0 + scale) + shift).astype(BF16)

    half = xn_ref.shape[0] // 2
    halves = [slice(0, half), slice(half, 2 * half)]
    accs = [jnp.dot(xn_ref[r, :], w_ref[...], preferred_element_type=F32) for r in halves]
    for r, acc in zip(halves, accs):
        o_ref[r, :] = acc.astype(o_ref.dtype)
    if not rope:
        return

    n_heads = tn // HEAD
    q_tiles = P_AG // tn
    kv_tile = P_K // tn
    k_heads = (P_V - P_K) // HEAD
    q_scale = HEAD ** -0.5 * LOG2_E

    def store_rope(n_rot, mult):
        for r, acc in zip(halves, accs):
            for h in range(n_rot):
                sl = slice(h * HEAD, (h + 1) * HEAD)
                rot = _rope(acc[:, sl], cos_ref[r, :], slo_ref[r, :], shi_ref[r, :])
                if mult is not None:
                    rot = rot * mult
                o_ref[r, sl] = rot.astype(o_ref.dtype)

    @pl.when(j < q_tiles)
    def _():
        store_rope(n_heads, q_scale)

    @pl.when(j == kv_tile)
    def _():
        store_rope(k_heads, None)


def _inproj_latent(x2, mod3, gain, cos, slo, shi, w_bf, seq):
    m = x2.shape[0]
    tm, tn = 1024, 1024
    tiles_per_seq = seq // tm
    n_tiles = IN_COLS // tn

    def w_map(i, j):
        src = jnp.where(j < 2, j, jnp.where(j == n_tiles - 1, 2, j + 1))
        return (0, src)

    return pl.pallas_call(
        functools.partial(_inproj_kernel, rope=True, tn=tn),
        grid=(m // tm, n_tiles),
        in_specs=[pl.BlockSpec((tm, D_MODEL), lambda i, j: (i, 0)),
                  pl.BlockSpec((1, 1, 3 * D_MODEL), lambda i, j: (i // tiles_per_seq, 0, 0)),
                  pl.BlockSpec((1, D_MODEL), lambda i, j: (0, 0)),
                  pl.BlockSpec((tm, HEAD), lambda i, j: (i % tiles_per_seq, 0)),
                  pl.BlockSpec((tm, HEAD), lambda i, j: (i % tiles_per_seq, 0)),
                  pl.BlockSpec((tm, HEAD), lambda i, j: (i % tiles_per_seq, 0)),
                  pl.BlockSpec((D_MODEL, tn), w_map)],
        out_specs=pl.BlockSpec((tm, tn), lambda i, j: (i, j)),
        out_shape=jax.ShapeDtypeStruct((m, IN_COLS), BF16),
        scratch_shapes=[pltpu.VMEM((tm, D_MODEL), BF16)],
        compiler_params=pltpu.CompilerParams(
            dimension_semantics=("arbitrary", "arbitrary"), vmem_limit_bytes=VMEM_LIMIT),
        name="inproj",
    )(x2, mod3, gain, cos, slo, shi, w_bf)


def _inproj_ctx(ctx2, modc3, gain, cos, slo, shi, w_bf):
    m = ctx2.shape[0]
    tn = 1024
    n_tiles = C_COLS // tn

    def w_map(i, j):
        return (0, jnp.where(j == 0, 2, j + 6))

    return pl.pallas_call(
        functools.partial(_inproj_kernel, rope=False, tn=tn),
        grid=(1, n_tiles),
        in_specs=[pl.BlockSpec((m, D_MODEL), lambda i, j: (0, 0)),
                  pl.BlockSpec((1, 1, 3 * D_MODEL), lambda i, j: (0, 0, 0)),
                  pl.BlockSpec((1, D_MODEL), lambda i, j: (0, 0)),
                  pl.BlockSpec((m, HEAD), lambda i, j: (0, 0)),
                  pl.BlockSpec((m, HEAD), lambda i, j: (0, 0)),
                  pl.BlockSpec((m, HEAD), lambda i, j: (0, 0)),
                  pl.BlockSpec((D_MODEL, tn), w_map)],
        out_specs=pl.BlockSpec((m, tn), lambda i, j: (0, j)),
        out_shape=jax.ShapeDtypeStruct((m, C_COLS), BF16),
        scratch_shapes=[pltpu.VMEM((m, D_MODEL), BF16)],
        compiler_params=pltpu.CompilerParams(
            dimension_semantics=("arbitrary", "arbitrary"), vmem_limit_bytes=VMEM_LIMIT),
        name="inproj_ctx",
    )(ctx2, modc3, gain, cos, slo, shi, w_bf)


def _attn_kernel(sink_ref, q_ref, ag_ref, kp_ref, kc_ref, kn_ref, vp_ref, vc_ref, vn_ref,
                 kx_ref, vx_ref, o_ref, *, n_blocks):
    i = pl.program_id(1)
    qb = q_ref.shape[0]
    row = lax.broadcasted_iota(jnp.int32, (ATT_GROUP * qb, qb), 0) & (qb - 1)
    col = lax.broadcasted_iota(jnp.int32, (ATT_GROUP * qb, qb), 1)
    ok_prev = col >= row + jnp.where(i > 0, 0, 2 * qb)
    ok_next = col + jnp.where(i < n_blocks - 1, 0, 2 * qb) <= row

    def scores(kh):
        ks = slice(kh * HEAD, (kh + 1) * HEAD)
        kcat = jnp.concatenate([kp_ref[:, ks], kc_ref[:, ks], kn_ref[:, ks], kx_ref[:, ks]], axis=0)
        q4 = jnp.concatenate(
            [q_ref[:, (kh * ATT_GROUP + g) * HEAD:(kh * ATT_GROUP + g + 1) * HEAD]
             for g in range(ATT_GROUP)], axis=0)
        return _dot_nt(q4, kcat)

    def finish(kh, s):
        ks = slice(kh * HEAD, (kh + 1) * HEAD)
        vcat = jnp.concatenate([vp_ref[:, ks], vc_ref[:, ks], vn_ref[:, ks], vx_ref[:, ks]], axis=0)
        heads = [kh * ATT_GROUP + g for g in range(ATT_GROUP)]
        s = jnp.concatenate(
            [jnp.where(ok_prev, s[:, 0:qb], -jnp.inf), s[:, qb:2 * qb],
             jnp.where(ok_next, s[:, 2 * qb:3 * qb], -jnp.inf), s[:, 3 * qb:]], axis=1)
        sink = jnp.concatenate(
            [jnp.full((qb, 1), sink_ref[0, h] * LOG2_E, F32) for h in heads], axis=0)
        m = jnp.maximum(jnp.max(s, axis=-1, keepdims=True), sink)
        p = jnp.exp2(s - m)
        den = jnp.sum(p, axis=-1, keepdims=True) + jnp.exp2(sink - m)
        o4 = jnp.dot(p.astype(BF16), vcat, preferred_element_type=F32) / den
        for g, h in enumerate(heads):
            hs = slice(h * HEAD, (h + 1) * HEAD)
            gate = _silu(ag_ref[:, hs].astype(F32))
            o_ref[:, hs] = (o4[g * qb:(g + 1) * qb, :] * gate).astype(o_ref.dtype)

    pending = scores(0)
    for kh in range(ATT_KV_HEADS):
        upcoming = scores(kh + 1) if kh + 1 < ATT_KV_HEADS else None
        finish(kh, pending)
        pending = upcoming


def _attention(p_lat, p_ctx, sink, batch, seq, n_ctx):
    qb = WINDOW
    nb = seq // qb
    kvw = ATT_KV_HEADS * HEAD
    kcol, vcol = P_K // kvw, P_V // kvw

    def rows(off):
        return lambda b, i: b * nb + jnp.clip(i + off, 0, nb - 1)

    def spec(w, rfn, c):
        return pl.BlockSpec((qb, w), lambda b, i: (rfn(b, i), c))

    return pl.pallas_call(
        functools.partial(_attn_kernel, n_blocks=nb),
        grid=(batch, nb),
        in_specs=[pl.BlockSpec(memory_space=pltpu.SMEM),
                  spec(D_MODEL, rows(0), P_Q // D_MODEL),
                  spec(D_MODEL, rows(0), P_AG // D_MODEL),
                  spec(kvw, rows(-1), kcol), spec(kvw, rows(0), kcol), spec(kvw, rows(1), kcol),
                  spec(kvw, rows(-1), vcol), spec(kvw, rows(0), vcol), spec(kvw, rows(1), vcol),
                  pl.BlockSpec((n_ctx, kvw), lambda b, i: (b, C_K // kvw)),
                  pl.BlockSpec((n_ctx, kvw), lambda b, i: (b, C_V // kvw))],
        out_specs=pl.BlockSpec((qb, D_MODEL), lambda b, i: (b * nb + i, 0)),
        out_shape=jax.ShapeDtypeStruct((batch * seq, D_MODEL), BF16),
        compiler_params=pltpu.CompilerParams(
            dimension_semantics=("arbitrary", "arbitrary"), vmem_limit_bytes=VMEM_LIMIT),
        name="attn",
    )(sink, p_lat, p_lat, p_lat, p_lat, p_lat, p_lat, p_lat, p_lat, p_ctx, p_ctx)


SUB = 16
N_SUB = CHUNK // SUB
ROW_FACTORS = 24


def _hgrn_decay_matrix(reverse):
    m = np.zeros((2 * CHUNK + 32, CHUNK), np.float32)
    r = 2 * CHUNK
    for i in range(N_SUB):
        lo, hi, mid = i * SUB, (i + 1) * SUB, i * SUB + SUB // 2
        for t in range(lo, hi):
            if reverse:
                m[t, t:hi] = 1
                m[CHUNK + t, lo:t] = 1
            else:
                m[t, lo:t + 1] = 1
                m[CHUNK + t, t + 1:hi] = 1
        if reverse:
            before, after, first, second = slice(hi, CHUNK), slice(0, lo), slice(mid, hi), slice(lo, mid)
        else:
            before, after, first, second = slice(0, lo), slice(hi, CHUNK), slice(lo, mid), slice(mid, hi)
        m[r + i, before] = 1
        m[r + 4 + i, after] = 1
        m[r + 8 + i, first] = -1
        m[r + 12 + i, second] = -1
    b2, b1 = (1, 2) if reverse else (2, 1)
    m[r + 16, b2 * SUB:(b2 + 1) * SUB] = 1
    m[r + 17, b1 * SUB:(b1 + 1) * SUB] = 1
    m[r + 18, :] = 1
    return np.concatenate([m, m], axis=1)


def _cumsum_rows(tri_bf, g):
    g1 = g.astype(BF16)
    r1 = g - g1.astype(F32)
    g2 = r1.astype(BF16)
    g3 = (r1 - g2.astype(F32)).astype(BF16)
    dot = functools.partial(jnp.dot, preferred_element_type=F32)
    return dot(tri_bf, g1) + dot(tri_bf, g2) + dot(tri_bf, g3)


def _hgrn_kernel(lbl_ref, dm_ref, cz_ref, cv_ref, q_ref, z_ref, v_ref, *rest, reverse, finalize,
                 heads, chunks_per_step):
    if finalize:
        hg_ref, ob_ref, gain_ref, o_ref, st_ref = rest
    else:
        o_ref, st_ref = rest
    t = pl.program_id(2)
    c = CHUNK
    row = lax.broadcasted_iota(jnp.int32, (c, c), 0)
    col = lax.broadcasted_iota(jnp.int32, (c, c), 1)
    tri = (col >= row) if reverse else (col <= row)
    end = 0 if reverse else c - 1

    l0, l1 = lbl_ref[0:1, :], lbl_ref[1:2, :]
    lm = jnp.maximum(l0, l1)
    e0, e1 = jnp.exp(l0 - lm), jnp.exp(l1 - lm)
    lb = e0 / (e0 + e1)
    fa, fb = lb + 0.5 * (1.0 - lb), 0.5 * (1.0 - lb)

    def forget(z):
        f = fa + fb * jnp.tanh(0.5 * z.astype(F32))
        return jnp.log(f), 1.0 - f

    @pl.when(t == 0)
    def _():
        tri_bf = jnp.where(tri, 1.0, 0.0).astype(BF16)
        n_cc = cz_ref.shape[0] // c
        s_ctx = [jnp.zeros((HEAD, HEAD), F32) for _ in range(heads)]
        for ci in (range(n_cc - 1, -1, -1) if reverse else range(n_cc)):
            rs = slice(ci * c, (ci + 1) * c)
            g, kk = forget(cz_ref[rs, :])
            gc = _cumsum_rows(tri_bf, g)
            for h in range(heads):
                hs = slice(h * HEAD, (h + 1) * HEAD)
                gend = gc[end:end + 1, hs]
                ks = (kk[:, hs] * jnp.exp(gend - gc[:, hs])).astype(BF16)
                s_ctx[h] = jnp.exp(gend) * s_ctx[h] + _dot_tn(cv_ref[rs, hs], ks)
        for h in range(heads):
            st_ref[h] = s_ctx[h]

    def blk(x, i):
        return x[i * SUB:(i + 1) * SUB]

    def tb(tau):
        return N_SUB - 1 - tau if reverse else tau

    zero = jnp.zeros((SUB, HEAD), BF16)

    def place(parts):
        return jnp.concatenate([parts.get(i, zero) for i in range(N_SUB)], axis=0)

    diag = jnp.logical_and(tri, (row ^ col) < SUB)

    def scores(ci):
        rs = slice(ci * c, (ci + 1) * c)
        g, kk = forget(z_ref[rs, :])
        g1 = g.astype(BF16)
        g2 = (g - g1.astype(F32)).astype(BF16)
        ld = jnp.dot(dm_ref[...], jnp.concatenate([g1, g2], axis=0), preferred_element_type=F32)
        hq = 0.5 * q_ref[rs, :].astype(F32)
        q_hat = (hq + hq * jnp.tanh(hq)) * jnp.exp(ld[0:c])
        k_hat = kk * jnp.exp(ld[c:2 * c])
        rf = jnp.exp(ld[2 * c:2 * c + ROW_FACTORS])
        per_head = []
        for h in range(heads):
            hs = slice(h * HEAD, (h + 1) * HEAD)
            q_h, k_h = q_hat[:, hs], k_hat[:, hs]

            def scaled(x, r0):
                return jnp.concatenate(
                    [blk(x, i) * rf[r0 + i:r0 + i + 1, hs] for i in range(N_SUB)],
                    axis=0).astype(BF16)

            qi, ks, qd, kd = scaled(q_h, 0), scaled(k_h, 4), scaled(q_h, 8), scaled(k_h, 12)
            qb, kb = q_h.astype(BF16), k_h.astype(BF16)
            q3 = (blk(q_h, tb(3)) * rf[16:17, hs]).astype(BF16)
            k0 = (blk(k_h, tb(0)) * rf[17:18, hs]).astype(BF16)
            ql = jnp.concatenate(
                [place({tb(1): blk(qb, tb(1))}), place({tb(3): blk(qb, tb(3))}),
                 place({tb(2): blk(qb, tb(2)), tb(3): q3})], axis=1)
            kl = jnp.concatenate(
                [place({tb(0): blk(kb, tb(0))}), place({tb(2): blk(kb, tb(2))}),
                 place({tb(1): blk(kb, tb(1)), tb(0): k0})], axis=1)
            per_head.append((qi, ks, _dot_nt(ql, kl), _dot_nt(qd, kd), rf[18:19, hs]))
        return per_head

    def outputs(ci, per_head, state):
        rs = slice(ci * c, (ci + 1) * c)
        a_bf = [jnp.where(diag, a_dg, a_off).astype(BF16) for _, _, a_off, a_dg, _ in per_head]
        outs, new_state = [], []
        for h in range(heads):
            hs = slice(h * HEAD, (h + 1) * HEAD)
            qi, ks, _, _, dec = per_head[h]
            v_h = v_ref[rs, hs]
            outs.append(_dot_nt(qi, state[h].astype(BF16))
                        + jnp.dot(a_bf[h], v_h, preferred_element_type=F32))
            new_state.append(dec * state[h] + _dot_tn(v_h, ks))
        for h in range(heads):
            hs = slice(h * HEAD, (h + 1) * HEAD)
            if finalize:
                ot = outs[h] + ob_ref[rs, hs]
                y = ot * lax.rsqrt(jnp.mean(ot * ot, axis=-1, keepdims=True) + EPS)
                y = y * gain_ref[:, hs]
                o_ref[rs, hs] = (y * _silu(hg_ref[rs, hs].astype(F32))).astype(o_ref.dtype)
            else:
                o_ref[rs, hs] = outs[h]
        return new_state

    state = [st_ref[h] for h in range(heads)]
    order = list(range(chunks_per_step - 1, -1, -1) if reverse else range(chunks_per_step))
    pending = scores(order[0])
    for n, ci in enumerate(order):
        upcoming = scores(order[n + 1]) if n + 1 < len(order) else None
        state = outputs(ci, pending, state)
        pending = upcoming
    for h in range(heads):
        st_ref[h] = state[h]


def _hgrn_pass(p_lat, p_ctx, lb_logits, o_back, gain, batch, seq, n_ctx, *, reverse):
    finalize = o_back is not None
    heads = 8
    hw = heads * HEAD
    cps = 4
    tb = cps * CHUNK
    steps = seq // tb
    hblocks = HG_HEADS // heads
    dm = jnp.asarray(_hgrn_decay_matrix(reverse), BF16)

    def rblk(b, t):
        return b * steps + ((steps - 1 - t) if reverse else t)

    def lat(col0):
        return pl.BlockSpec((tb, hw), lambda b, h, t: (rblk(b, t), col0 // hw + h))

    z_lat, z_ctx = (P_HFB, C_HFB) if reverse else (P_HFF, C_HFF)
    in_specs = [pl.BlockSpec((2, hw), lambda b, h, t: (0, h)),
                pl.BlockSpec(dm.shape, lambda b, h, t: (0, 0)),
                pl.BlockSpec((n_ctx, hw), lambda b, h, t: (b, z_ctx // hw + h)),
                pl.BlockSpec((n_ctx, hw), lambda b, h, t: (b, C_HI // hw + h)),
                lat(P_HQ), lat(z_lat), lat(P_HI)]
    args = [lb_logits, dm, p_ctx, p_ctx, p_lat, p_lat, p_lat]
    if finalize:
        in_specs += [lat(P_HG), lat(0), pl.BlockSpec((1, hw), lambda b, h, t: (0, h))]
        args += [p_lat, o_back, gain]
    return pl.pallas_call(
        functools.partial(_hgrn_kernel, reverse=reverse, finalize=finalize, heads=heads,
                          chunks_per_step=cps),
        grid=(batch, hblocks, steps),
        in_specs=in_specs,
        out_specs=lat(0),
        out_shape=jax.ShapeDtypeStruct((batch * seq, HG_HEADS * HEAD), BF16 if finalize else F32),
        scratch_shapes=[pltpu.VMEM((heads, HEAD, HEAD), F32)],
        compiler_params=pltpu.CompilerParams(
            dimension_semantics=("arbitrary", "arbitrary", "arbitrary"),
            vmem_limit_bytes=VMEM_LIMIT),
        name="hgrn_fwd" if finalize else "hgrn_bwd",
    )(*args)


def _merge_kernel(hg_ref, at_ref, wh_ref, wa_ref, ma_ref, mb_ref, bm_ref, o_ref):
    half = hg_ref.shape[0] // 2
    halves = [slice(0, half), slice(half, 2 * half)]
    prods = [(jnp.dot(hg_ref[r, :], wh_ref[...], preferred_element_type=F32),
              jnp.dot(at_ref[r, :], wa_ref[...], preferred_element_type=F32)) for r in halves]
    for r, (ya, yb) in zip(halves, prods):
        ga = jax.nn.sigmoid(ma_ref[r, :].astype(F32) + bm_ref[0:1, :])
        gb = jax.nn.sigmoid(mb_ref[r, :].astype(F32) + bm_ref[1:2, :])
        o_ref[r, :] = (ga * ya + gb * yb).astype(o_ref.dtype)


def _merge(hg_branch, att_branch, wh_bf, wa_bf, p_lat, b_merge):
    m = hg_branch.shape[0]
    tm, tn = 1024, 512
    return pl.pallas_call(
        _merge_kernel,
        grid=(m // tm, D_MODEL // tn),
        in_specs=[pl.BlockSpec((tm, D_MODEL), lambda i, j: (i, 0)),
                  pl.BlockSpec((tm, D_MODEL), lambda i, j: (i, 0)),
                  pl.BlockSpec((D_MODEL, tn), lambda i, j: (0, j)),
                  pl.BlockSpec((D_MODEL, tn), lambda i, j: (0, j)),
                  pl.BlockSpec((tm, tn), lambda i, j: (i, P_MA // tn + j)),
                  pl.BlockSpec((tm, tn), lambda i, j: (i, P_MB // tn + j)),
                  pl.BlockSpec((2, tn), lambda i, j: (0, j))],
        out_specs=pl.BlockSpec((tm, tn), lambda i, j: (i, j)),
        out_shape=jax.ShapeDtypeStruct((m, D_MODEL), BF16),
        compiler_params=pltpu.CompilerParams(
            dimension_semantics=("arbitrary", "arbitrary"), vmem_limit_bytes=VMEM_LIMIT),
        name="merge",
    )(hg_branch, att_branch, wh_bf, wa_bf, p_lat, p_lat, b_merge)


def _out_kernel(y_ref, w_ref, x_ref, mod_ref, gain_ref, o_ref):
    half = y_ref.shape[0] // 2
    halves = [slice(0, half), slice(half, 2 * half)]
    zs = [jnp.dot(y_ref[r, :], w_ref[...], preferred_element_type=F32) for r in halves]
    gate = mod_ref[0, :, 2 * D_MODEL:3 * D_MODEL]
    for r, z in zip(halves, zs):
        h = x_ref[r, :] + gate * z
        y = h * lax.rsqrt(jnp.mean(h * h, axis=-1, keepdims=True) + EPS)
        o_ref[r, :] = y * gain_ref[...]


def _out(y, wo_bf, x2, mod3, fgain, seq):
    m = y.shape[0]
    tm = 512
    tiles_per_seq = seq // tm
    return pl.pallas_call(
        _out_kernel,
        grid=(m // tm,),
        in_specs=[pl.BlockSpec((tm, D_MODEL), lambda i: (i, 0)),
                  pl.BlockSpec((D_MODEL, D_MODEL), lambda i: (0, 0)),
                  pl.BlockSpec((tm, D_MODEL), lambda i: (i, 0)),
                  pl.BlockSpec((1, 1, 3 * D_MODEL), lambda i: (i // tiles_per_seq, 0, 0)),
                  pl.BlockSpec((1, D_MODEL), lambda i: (0, 0))],
        out_specs=pl.BlockSpec((tm, D_MODEL), lambda i: (i, 0)),
        out_shape=jax.ShapeDtypeStruct((m, D_MODEL), F32),
        compiler_params=pltpu.CompilerParams(
            dimension_semantics=("arbitrary",), vmem_limit_bytes=VMEM_LIMIT),
        name="out",
    )(y, wo_bf, x2, mod3, fgain)


def _rope_tables(seq):
    rows = seq // GRID_W
    row = jnp.repeat(jnp.arange(rows, dtype=F32), GRID_W)
    col = jnp.tile(jnp.arange(GRID_W, dtype=F32), rows)
    half = HEAD // 2
    inv_freq = 1.0 / (ROPE_BASE ** (jnp.arange(0, half, 2, dtype=F32) / half))
    ang_r = row[:, None] * inv_freq[None, :]
    ang_c = col[:, None] * inv_freq[None, :]
    ang = jnp.concatenate([ang_r, ang_r, ang_c, ang_c], axis=-1)
    cos, sin = jnp.cos(ang), jnp.sin(ang)
    lo = (jnp.arange(HEAD) % half) < (half // 2)
    return cos, jnp.where(lo, -sin, 0.0), jnp.where(lo, 0.0, sin)


def kernel(x, c, ctx, c_ctx, w_ada, b_ada, norm_gain, w_in, b_merge, lb_logits_fwd,
           lb_logits_bwd, hgrn_norm_gain, w_o_hgrn, sink_logits, w_o_attn, w_out,
           final_norm_gain):
    batch, seq, d = x.shape
    n_ctx = ctx.shape[1]
    assert d == D_MODEL and w_ada.shape[0] == 1 and seq % 1024 == 0 and batch * n_ctx == 1024

    c8 = jnp.concatenate([c, c_ctx[None, :], jnp.zeros((8 - batch - 1, d), F32)], axis=0)
    mod8 = _ada(c8, w_ada[0], b_ada[0][None, :])
    mod3 = mod8[:batch, None, :]
    modc3 = mod8[batch:batch + 1, None, :]

    cos, slo, shi = _rope_tables(seq)
    w_bf = w_in[0].astype(BF16)
    gain = norm_gain[0][None, :]
    x2 = x.reshape(batch * seq, d)
    p_lat = _inproj_latent(x2, mod3, gain, cos, slo, shi, w_bf, seq)
    p_ctx = _inproj_ctx(ctx.reshape(batch * n_ctx, d), modc3, gain,
                        cos[:batch * n_ctx], slo[:batch * n_ctx], shi[:batch * n_ctx], w_bf)

    att_branch = _attention(p_lat, p_ctx, sink_logits[0][None, :], batch, seq, n_ctx)

    o_back = _hgrn_pass(p_lat, p_ctx, lb_logits_bwd, None, None, batch, seq, n_ctx, reverse=True)
    hg_branch = _hgrn_pass(p_lat, p_ctx, lb_logits_fwd, o_back, hgrn_norm_gain[0][None, :],
                           batch, seq, n_ctx, reverse=False)

    y = _merge(hg_branch, att_branch, w_o_hgrn[0].astype(BF16), w_o_attn[0].astype(BF16),
               p_lat, b_merge[0])
    out = _out(y, w_out[0].astype(BF16), x2, mod3, final_norm_gain[None, :], seq)
    return out.reshape(batch, seq, d)
```

```python
import functools

import jax
import jax.numpy as jnp
import numpy as np
from jax import lax
from jax.experimental import pallas as pl
from jax.experimental.pallas import tpu as pltpu

F32 = jnp.float32
BF16 = jnp.bfloat16

D_MODEL = 2048
GRID_W = 64
EPS = 1e-6
HEAD = 128
ATT_HEADS = 16
ATT_KV_HEADS = 4
ATT_GROUP = ATT_HEADS // ATT_KV_HEADS
WINDOW = 128
ROPE_BASE = 10000.0
LOG2_E = 1.4426950408889634
HG_HEADS = 16
CHUNK = 64
IN_COLS = 19456

P_Q, P_AG, P_HQ, P_HFF, P_HFB, P_HI, P_HG, P_MA, P_MB, P_K, P_V = (
    0, 2048, 4096, 6144, 8192, 10240, 12288, 14336, 16384, 18432, 18944)
C_K, C_V, C_HFF, C_HFB, C_HI, C_COLS = 0, 512, 1024, 3072, 5120, 7168

VMEM_LIMIT = 56 * 1024 * 1024


def _silu(x):
    h = 0.5 * x
    return h + h * jnp.tanh(h)


def _dot_nt(a, b):
    return lax.dot_general(a, b, (((1,), (1,)), ((), ())), preferred_element_type=F32)


def _dot_tn(a, b):
    return lax.dot_general(a, b, (((0,), (0,)), ((), ())), preferred_element_type=F32)


def _ada_kernel(c_ref, w_ref, b_ref, o_ref):
    s = _silu(c_ref[...])
    o_ref[...] = jnp.dot(s.astype(BF16), w_ref[...].astype(BF16),
                         preferred_element_type=F32) + b_ref[...]


def _ada(c8, w_ada, b_ada):
    n_out = w_ada.shape[1]
    tn = 1024
    return pl.pallas_call(
        _ada_kernel,
        grid=(n_out // tn,),
        in_specs=[pl.BlockSpec((8, D_MODEL), lambda j: (0, 0)),
                  pl.BlockSpec((D_MODEL, tn), lambda j: (0, j)),
                  pl.BlockSpec((1, tn), lambda j: (0, j))],
        out_specs=pl.BlockSpec((8, tn), lambda j: (0, j)),
        out_shape=jax.ShapeDtypeStruct((8, n_out), F32),
        compiler_params=pltpu.CompilerParams(
            dimension_semantics=("arbitrary",), vmem_limit_bytes=VMEM_LIMIT),
        name="ada",
    )(c8, w_ada, b_ada)


def _rope(t, cos, sin_lo, sin_hi):
    return t * cos + pltpu.roll(t, 96, 1) * sin_lo + pltpu.roll(t, 32, 1) * sin_hi


def _norm_modulate(x_ref, mod_ref, gain_ref):
    x = x_ref[...]
    y = x * lax.rsqrt(jnp.mean(x * x, axis=-1, keepdims=True) + EPS) * gain_ref[...]
    shift = mod_ref[0, :, 0:D_MODEL]
    scale = mod_ref[0, :, D_MODEL:2 * D_MODEL]
    return (y * (1.0 + scale) + shift).astype(BF16)


def _inproj_kernel(x_ref, mod_ref, gain_ref, cos_ref, slo_ref, shi_ref, w_ref, o_ref, xn_ref,
                   *, tn):
    j = pl.program_id(1)

    @pl.when(j == 0)
    def _():
        xn_ref[...] = _norm_modulate(x_ref, mod_ref, gain_ref)

    half = xn_ref.shape[0] // 2
    halves = [slice(0, half), slice(half, 2 * half)]
    accs = [jnp.dot(xn_ref[r, :], w_ref[...], preferred_element_type=F32) for r in halves]
    for r, acc in zip(halves, accs):
        o_ref[r, :] = acc.astype(o_ref.dtype)

    n_heads = tn // HEAD
    q_tiles = P_AG // tn
    kv_tile = P_K // tn
    k_heads = (P_V - P_K) // HEAD
    q_scale = HEAD ** -0.5 * LOG2_E

    def store_rope(n_rot, mult):
        for r, acc in zip(halves, accs):
            for h in range(n_rot):
                sl = slice(h * HEAD, (h + 1) * HEAD)
                rot = _rope(acc[:, sl], cos_ref[r, :], slo_ref[r, :], shi_ref[r, :])
                if mult is not None:
                    rot = rot * mult
                o_ref[r, sl] = rot.astype(o_ref.dtype)

    @pl.when(j < q_tiles)
    def _():
        store_rope(n_heads, q_scale)

    @pl.when(j == kv_tile)
    def _():
        store_rope(k_heads, None)


def _inproj_latent(x2, mod3, gain, cos, slo, shi, w_bf, seq):
    m = x2.shape[0]
    tm, tn = 1024, 1024
    tiles_per_seq = seq // tm
    n_tiles = IN_COLS // tn

    def w_map(i, j):
        src = jnp.where(j < 2, j, jnp.where(j == n_tiles - 1, 2, j + 1))
        return (0, src)

    return pl.pallas_call(
        functools.partial(_inproj_kernel, tn=tn),
        grid=(m // tm, n_tiles),
        in_specs=[pl.BlockSpec((tm, D_MODEL), lambda i, j: (i, 0)),
                  pl.BlockSpec((1, 1, 3 * D_MODEL), lambda i, j: (i // tiles_per_seq, 0, 0)),
                  pl.BlockSpec((1, D_MODEL), lambda i, j: (0, 0)),
                  pl.BlockSpec((tm, HEAD), lambda i, j: (i % tiles_per_seq, 0)),
                  pl.BlockSpec((tm, HEAD), lambda i, j: (i % tiles_per_seq, 0)),
                  pl.BlockSpec((tm, HEAD), lambda i, j: (i % tiles_per_seq, 0)),
                  pl.BlockSpec((D_MODEL, tn), w_map)],
        out_specs=pl.BlockSpec((tm, tn), lambda i, j: (i, j)),
        out_shape=jax.ShapeDtypeStruct((m, IN_COLS), BF16),
        scratch_shapes=[pltpu.VMEM((tm, D_MODEL), BF16)],
        compiler_params=pltpu.CompilerParams(
            dimension_semantics=("arbitrary", "arbitrary"), vmem_limit_bytes=VMEM_LIMIT),
        name="inproj",
    )(x2, mod3, gain, cos, slo, shi, w_bf)


CTX_TN = 512
W_K, W_KV_END, W_FF, W_I_END = 2048, 3072, 7168, 13312
CTX_KV_TILES = (W_K // CTX_TN, W_KV_END // CTX_TN)
CTX_HG_TILES = (W_FF // CTX_TN, W_I_END // CTX_TN)
CTX_HG_SLOT0 = C_HFF // CTX_TN


def _ctx_tile_needed(j):
    return jnp.logical_or(jnp.logical_and(j >= CTX_KV_TILES[0], j < CTX_KV_TILES[1]),
                          jnp.logical_and(j >= CTX_HG_TILES[0], j < CTX_HG_TILES[1]))


def _ctx_out_slot(j):
    kv = jnp.clip(j - CTX_KV_TILES[0], 0, CTX_KV_TILES[1] - CTX_KV_TILES[0] - 1)
    hg = jnp.clip(j - CTX_HG_TILES[0], 0, CTX_HG_TILES[1] - CTX_HG_TILES[0] - 1) + CTX_HG_SLOT0
    return jnp.where(j < CTX_HG_TILES[0], kv, hg)


def _wcast_ctx_kernel(x_ref, mod_ref, gain_ref, w_ref, wb_ref, o_ref, xn_ref):
    j = pl.program_id(0)

    @pl.when(j == 0)
    def _():
        xn_ref[...] = _norm_modulate(x_ref, mod_ref, gain_ref)

    wb = w_ref[...].astype(BF16)
    wb_ref[...] = wb

    @pl.when(_ctx_tile_needed(j))
    def _():
        o_ref[...] = jnp.dot(xn_ref[...], wb, preferred_element_type=F32).astype(o_ref.dtype)


def _wcast_ctx(ctx2, modc3, gain, w_in):
    m = ctx2.shape[0]
    tn = CTX_TN
    return pl.pallas_call(
        _wcast_ctx_kernel,
        grid=(IN_COLS // tn,),
        in_specs=[pl.BlockSpec((m, D_MODEL), lambda j: (0, 0)),
                  pl.BlockSpec((1, 1, 3 * D_MODEL), lambda j: (0, 0, 0)),
                  pl.BlockSpec((1, D_MODEL), lambda j: (0, 0)),
                  pl.BlockSpec((D_MODEL, tn), lambda j: (0, j))],
        out_specs=[pl.BlockSpec((D_MODEL, tn), lambda j: (0, j)),
                   pl.BlockSpec((m, tn), lambda j: (0, _ctx_out_slot(j)))],
        out_shape=[jax.ShapeDtypeStruct((D_MODEL, IN_COLS), BF16),
                   jax.ShapeDtypeStruct((m, C_COLS), BF16)],
        scratch_shapes=[pltpu.VMEM((m, D_MODEL), BF16)],
        compiler_params=pltpu.CompilerParams(
            dimension_semantics=("arbitrary",), vmem_limit_bytes=VMEM_LIMIT),
        name="wcast_ctx",
    )(ctx2, modc3, gain, w_in)


def _attn_kernel(sink_ref, q_ref, ag_ref, kp_ref, kc_ref, kn_ref, vp_ref, vc_ref, vn_ref,
                 kx_ref, vx_ref, o_ref, *, n_blocks):
    i = pl.program_id(1)
    qb = q_ref.shape[0]
    row = lax.broadcasted_iota(jnp.int32, (ATT_GROUP * qb, qb), 0) & (qb - 1)
    col = lax.broadcasted_iota(jnp.int32, (ATT_GROUP * qb, qb), 1)
    ok_prev = col >= row + jnp.where(i > 0, 0, 2 * qb)
    ok_next = col + jnp.where(i < n_blocks - 1, 0, 2 * qb) <= row

    def scores(kh):
        ks = slice(kh * HEAD, (kh + 1) * HEAD)
        kcat = jnp.concatenate([kp_ref[:, ks], kc_ref[:, ks], kn_ref[:, ks], kx_ref[:, ks]], axis=0)
        q4 = jnp.concatenate(
            [q_ref[:, (kh * ATT_GROUP + g) * HEAD:(kh * ATT_GROUP + g + 1) * HEAD]
             for g in range(ATT_GROUP)], axis=0)
        return _dot_nt(q4, kcat)

    def finish(kh, s):
        ks = slice(kh * HEAD, (kh + 1) * HEAD)
        vcat = jnp.concatenate([vp_ref[:, ks], vc_ref[:, ks], vn_ref[:, ks], vx_ref[:, ks]], axis=0)
        heads = [kh * ATT_GROUP + g for g in range(ATT_GROUP)]
        s = jnp.concatenate(
            [jnp.where(ok_prev, s[:, 0:qb], -jnp.inf), s[:, qb:2 * qb],
             jnp.where(ok_next, s[:, 2 * qb:3 * qb], -jnp.inf), s[:, 3 * qb:]], axis=1)
        sink = jnp.concatenate(
            [jnp.full((qb, 1), sink_ref[0, h] * LOG2_E, F32) for h in heads], axis=0)
        m = jnp.maximum(jnp.max(s, axis=-1, keepdims=True), sink)
        p = jnp.exp2(s - m)
        den = jnp.sum(p, axis=-1, keepdims=True) + jnp.exp2(sink - m)
        o4 = jnp.dot(p.astype(BF16), vcat, preferred_element_type=F32) / den
        for g, h in enumerate(heads):
            hs = slice(h * HEAD, (h + 1) * HEAD)
            gate = _silu(ag_ref[:, hs].astype(F32))
            o_ref[:, hs] = (o4[g * qb:(g + 1) * qb, :] * gate).astype(o_ref.dtype)

    pending = scores(0)
    for kh in range(ATT_KV_HEADS):
        upcoming = scores(kh + 1) if kh + 1 < ATT_KV_HEADS else None
        finish(kh, pending)
        pending = upcoming


def _attention(p_lat, p_ctx, sink, batch, seq, n_ctx):
    qb = WINDOW
    nb = seq // qb
    kvw = ATT_KV_HEADS * HEAD
    kcol, vcol = P_K // kvw, P_V // kvw

    def rows(off):
        return lambda b, i: b * nb + jnp.clip(i + off, 0, nb - 1)

    def spec(w, rfn, c):
        return pl.BlockSpec((qb, w), lambda b, i: (rfn(b, i), c))

    return pl.pallas_call(
        functools.partial(_attn_kernel, n_blocks=nb),
        grid=(batch, nb),
        in_specs=[pl.BlockSpec(memory_space=pltpu.SMEM),
                  spec(D_MODEL, rows(0), P_Q // D_MODEL),
                  spec(D_MODEL, rows(0), P_AG // D_MODEL),
                  spec(kvw, rows(-1), kcol), spec(kvw, rows(0), kcol), spec(kvw, rows(1), kcol),
                  spec(kvw, rows(-1), vcol), spec(kvw, rows(0), vcol), spec(kvw, rows(1), vcol),
                  pl.BlockSpec((n_ctx, kvw), lambda b, i: (b, C_K // kvw)),
                  pl.BlockSpec((n_ctx, kvw), lambda b, i: (b, C_V // kvw))],
        out_specs=pl.BlockSpec((qb, D_MODEL), lambda b, i: (b * nb + i, 0)),
        out_shape=jax.ShapeDtypeStruct((batch * seq, D_MODEL), BF16),
        compiler_params=pltpu.CompilerParams(
            dimension_semantics=("arbitrary", "arbitrary"), vmem_limit_bytes=VMEM_LIMIT),
        name="attn",
    )(sink, p_lat, p_lat, p_lat, p_lat, p_lat, p_lat, p_lat, p_lat, p_ctx, p_ctx)


SUB = 16
N_SUB = CHUNK // SUB
ROW_FACTORS = 24


def _hgrn_decay_matrix(reverse):
    m = np.zeros((2 * CHUNK + 32, CHUNK), np.float32)
    r = 2 * CHUNK
    for i in range(N_SUB):
        lo, hi, mid = i * SUB, (i + 1) * SUB, i * SUB + SUB // 2
        for t in range(lo, hi):
            if reverse:
                m[t, t:hi] = 1
                m[CHUNK + t, lo:t] = 1
            else:
                m[t, lo:t + 1] = 1
                m[CHUNK + t, t + 1:hi] = 1
        if reverse:
            before, after, first, second = slice(hi, CHUNK), slice(0, lo), slice(mid, hi), slice(lo, mid)
        else:
            before, after, first, second = slice(0, lo), slice(hi, CHUNK), slice(lo, mid), slice(mid, hi)
        m[r + i, before] = 1
        m[r + 4 + i, after] = 1
        m[r + 8 + i, first] = -1
        m[r + 12 + i, second] = -1
    b2, b1 = (1, 2) if reverse else (2, 1)
    m[r + 16, b2 * SUB:(b2 + 1) * SUB] = 1
    m[r + 17, b1 * SUB:(b1 + 1) * SUB] = 1
    m[r + 18, :] = 1
    return np.concatenate([m, m], axis=1)


def _cumsum_rows(tri_bf, g):
    g1 = g.astype(BF16)
    r1 = g - g1.astype(F32)
    g2 = r1.astype(BF16)
    g3 = (r1 - g2.astype(F32)).astype(BF16)
    dot = functools.partial(jnp.dot, preferred_element_type=F32)
    return dot(tri_bf, g1) + dot(tri_bf, g2) + dot(tri_bf, g3)


def _hgrn_kernel(lbl_ref, dm_ref, cz_ref, cv_ref, q_ref, z_ref, v_ref, *rest, reverse, finalize,
                 heads, chunks_per_step):
    if finalize:
        hg_ref, ob_ref, gain_ref, o_ref, st_ref = rest
    else:
        o_ref, st_ref = rest
    t = pl.program_id(2)
    c = CHUNK
    row = lax.broadcasted_iota(jnp.int32, (c, c), 0)
    col = lax.broadcasted_iota(jnp.int32, (c, c), 1)
    tri = (col >= row) if reverse else (col <= row)
    end = 0 if reverse else c - 1

    l0, l1 = lbl_ref[0:1, :], lbl_ref[1:2, :]
    lm = jnp.maximum(l0, l1)
    e0, e1 = jnp.exp(l0 - lm), jnp.exp(l1 - lm)
    lb = e0 / (e0 + e1)
    fa, fb = lb + 0.5 * (1.0 - lb), 0.5 * (1.0 - lb)

    def forget(z):
        f = fa + fb * jnp.tanh(0.5 * z.astype(F32))
        return jnp.log(f), 1.0 - f

    @pl.when(t == 0)
    def _():
        tri_bf = jnp.where(tri, 1.0, 0.0).astype(BF16)
        n_cc = cz_ref.shape[0] // c
        s_ctx = [jnp.zeros((HEAD, HEAD), F32) for _ in range(heads)]
        for ci in (range(n_cc - 1, -1, -1) if reverse else range(n_cc)):
            rs = slice(ci * c, (ci + 1) * c)
            g, kk = forget(cz_ref[rs, :])
            gc = _cumsum_rows(tri_bf, g)
            for h in range(heads):
                hs = slice(h * HEAD, (h + 1) * HEAD)
                gend = gc[end:end + 1, hs]
                ks = (kk[:, hs] * jnp.exp(gend - gc[:, hs])).astype(BF16)
                s_ctx[h] = jnp.exp(gend) * s_ctx[h] + _dot_tn(cv_ref[rs, hs], ks)
        for h in range(heads):
            st_ref[h] = s_ctx[h]

    def blk(x, i):
        return x[i * SUB:(i + 1) * SUB]

    def tb(tau):
        return N_SUB - 1 - tau if reverse else tau

    zero = jnp.zeros((SUB, HEAD), BF16)

    def place(parts):
        return jnp.concatenate([parts.get(i, zero) for i in range(N_SUB)], axis=0)

    diag = jnp.logical_and(tri, (row ^ col) < SUB)

    def scores(ci):
        rs = slice(ci * c, (ci + 1) * c)
        g, kk = forget(z_ref[rs, :])
        g1 = g.astype(BF16)
        g2 = (g - g1.astype(F32)).astype(BF16)
        ld = jnp.dot(dm_ref[...], jnp.concatenate([g1, g2], axis=0), preferred_element_type=F32)
        hq = 0.5 * q_ref[rs, :].astype(F32)
        q_hat = (hq + hq * jnp.tanh(hq)) * jnp.exp(ld[0:c])
        k_hat = kk * jnp.exp(ld[c:2 * c])
        rf = jnp.exp(ld[2 * c:2 * c + ROW_FACTORS])
        per_head = []
        for h in range(heads):
            hs = slice(h * HEAD, (h + 1) * HEAD)
            q_h, k_h = q_hat[:, hs], k_hat[:, hs]

            def scaled(x, r0):
                return jnp.concatenate(
                    [blk(x, i) * rf[r0 + i:r0 + i + 1, hs] for i in range(N_SUB)],
                    axis=0).astype(BF16)

            qi, ks, qd, kd = scaled(q_h, 0), scaled(k_h, 4), scaled(q_h, 8), scaled(k_h, 12)
            qb, kb = q_h.astype(BF16), k_h.astype(BF16)
            q3 = (blk(q_h, tb(3)) * rf[16:17, hs]).astype(BF16)
            k0 = (blk(k_h, tb(0)) * rf[17:18, hs]).astype(BF16)
            ql = jnp.concatenate(
                [place({tb(1): blk(qb, tb(1))}), place({tb(3): blk(qb, tb(3))}),
                 place({tb(2): blk(qb, tb(2)), tb(3): q3})], axis=1)
            kl = jnp.concatenate(
                [place({tb(0): blk(kb, tb(0))}), place({tb(2): blk(kb, tb(2))}),
                 place({tb(1): blk(kb, tb(1)), tb(0): k0})], axis=1)
            per_head.append((qi, ks, _dot_nt(ql, kl), _dot_nt(qd, kd), rf[18:19, hs]))
        return per_head

    def outputs(ci, per_head, state):
        rs = slice(ci * c, (ci + 1) * c)
        a_bf = [jnp.where(diag, a_dg, a_off).astype(BF16) for _, _, a_off, a_dg, _ in per_head]
        outs, new_state = [], []
        for h in range(heads):
            hs = slice(h * HEAD, (h + 1) * HEAD)
            qi, ks, _, _, dec = per_head[h]
            v_h = v_ref[rs, hs]
            outs.append(_dot_nt(qi, state[h].astype(BF16))
                        + jnp.dot(a_bf[h], v_h, preferred_element_type=F32))
            new_state.append(dec * state[h] + _dot_tn(v_h, ks))
        for h in range(heads):
            hs = slice(h * HEAD, (h + 1) * HEAD)
            if finalize:
                ot = outs[h] + ob_ref[rs, hs]
                y = ot * lax.rsqrt(jnp.mean(ot * ot, axis=-1, keepdims=True) + EPS)
                y = y * gain_ref[:, hs]
                o_ref[rs, hs] = (y * _silu(hg_ref[rs, hs].astype(F32))).astype(o_ref.dtype)
            else:
                o_ref[rs, hs] = outs[h]
        return new_state

    state = [st_ref[h] for h in range(heads)]
    order = list(range(chunks_per_step - 1, -1, -1) if reverse else range(chunks_per_step))
    pending = scores(order[0])
    for n, ci in enumerate(order):
        upcoming = scores(order[n + 1]) if n + 1 < len(order) else None
        state = outputs(ci, pending, state)
        pending = upcoming
    for h in range(heads):
        st_ref[h] = state[h]


def _hgrn_pass(p_lat, p_ctx, lb_logits, o_back, gain, batch, seq, n_ctx, *, reverse):
    finalize = o_back is not None
    heads = 8
    hw = heads * HEAD
    cps = 8
    tb = cps * CHUNK
    steps = seq // tb
    hblocks = HG_HEADS // heads
    dm = jnp.asarray(_hgrn_decay_matrix(reverse), BF16)

    def rblk(b, t):
        return b * steps + ((steps - 1 - t) if reverse else t)

    def lat(col0):
        return pl.BlockSpec((tb, hw), lambda b, h, t: (rblk(b, t), col0 // hw + h))

    z_lat, z_ctx = (P_HFB, C_HFB) if reverse else (P_HFF, C_HFF)
    in_specs = [pl.BlockSpec((2, hw), lambda b, h, t: (0, h)),
                pl.BlockSpec(dm.shape, lambda b, h, t: (0, 0)),
                pl.BlockSpec((n_ctx, hw), lambda b, h, t: (b, z_ctx // hw + h)),
                pl.BlockSpec((n_ctx, hw), lambda b, h, t: (b, C_HI // hw + h)),
                lat(P_HQ), lat(z_lat), lat(P_HI)]
    args = [lb_logits, dm, p_ctx, p_ctx, p_lat, p_lat, p_lat]
    if finalize:
        in_specs += [lat(P_HG), lat(0), pl.BlockSpec((1, hw), lambda b, h, t: (0, h))]
        args += [p_lat, o_back, gain]
    return pl.pallas_call(
        functools.partial(_hgrn_kernel, reverse=reverse, finalize=finalize, heads=heads,
                          chunks_per_step=cps),
        grid=(batch, hblocks, steps),
        in_specs=in_specs,
        out_specs=lat(0),
        out_shape=jax.ShapeDtypeStruct((batch * seq, HG_HEADS * HEAD), BF16 if finalize else F32),
        scratch_shapes=[pltpu.VMEM((heads, HEAD, HEAD), F32)],
        compiler_params=pltpu.CompilerParams(
            dimension_semantics=("arbitrary", "arbitrary", "arbitrary"),
            vmem_limit_bytes=VMEM_LIMIT),
        name="hgrn_fwd" if finalize else "hgrn_bwd",
    )(*args)


def _merge_kernel(hg_ref, at_ref, wh_ref, wa_ref, ma_ref, mb_ref, bm_ref, o_ref):
    half = hg_ref.shape[0] // 2
    halves = [slice(0, half), slice(half, 2 * half)]
    prods = [(jnp.dot(hg_ref[r, :], wh_ref[...], preferred_element_type=F32),
              jnp.dot(at_ref[r, :], wa_ref[...], preferred_element_type=F32)) for r in halves]
    for r, (ya, yb) in zip(halves, prods):
        ga = jax.nn.sigmoid(ma_ref[r, :].astype(F32) + bm_ref[0:1, :])
        gb = jax.nn.sigmoid(mb_ref[r, :].astype(F32) + bm_ref[1:2, :])
        o_ref[r, :] = (ga * ya + gb * yb).astype(o_ref.dtype)


def _merge(hg_branch, att_branch, wh_bf, wa_bf, p_lat, b_merge):
    m = hg_branch.shape[0]
    tm, tn = 1024, 512
    return pl.pallas_call(
        _merge_kernel,
        grid=(m // tm, D_MODEL // tn),
        in_specs=[pl.BlockSpec((tm, D_MODEL), lambda i, j: (i, 0)),
                  pl.BlockSpec((tm, D_MODEL), lambda i, j: (i, 0)),
                  pl.BlockSpec((D_MODEL, tn), lambda i, j: (0, j)),
                  pl.BlockSpec((D_MODEL, tn), lambda i, j: (0, j)),
                  pl.BlockSpec((tm, tn), lambda i, j: (i, P_MA // tn + j)),
                  pl.BlockSpec((tm, tn), lambda i, j: (i, P_MB // tn + j)),
                  pl.BlockSpec((2, tn), lambda i, j: (0, j))],
        out_specs=pl.BlockSpec((tm, tn), lambda i, j: (i, j)),
        out_shape=jax.ShapeDtypeStruct((m, D_MODEL), BF16),
        compiler_params=pltpu.CompilerParams(
            dimension_semantics=("arbitrary", "arbitrary"), vmem_limit_bytes=VMEM_LIMIT),
        name="merge",
    )(hg_branch, att_branch, wh_bf, wa_bf, p_lat, p_lat, b_merge)


def _out_kernel(y_ref, w_ref, x_ref, mod_ref, gain_ref, o_ref):
    half = y_ref.shape[0] // 2
    halves = [slice(0, half), slice(half, 2 * half)]
    zs = [jnp.dot(y_ref[r, :], w_ref[...], preferred_element_type=F32) for r in halves]
    gate = mod_ref[0, :, 2 * D_MODEL:3 * D_MODEL]
    for r, z in zip(halves, zs):
        h = x_ref[r, :] + gate * z
        y = h * lax.rsqrt(jnp.mean(h * h, axis=-1, keepdims=True) + EPS)
        o_ref[r, :] = y * gain_ref[...]


def _out(y, wo_bf, x2, mod3, fgain, seq):
    m = y.shape[0]
    tm = 512
    tiles_per_seq = seq // tm
    return pl.pallas_call(
        _out_kernel,
        grid=(m // tm,),
        in_specs=[pl.BlockSpec((tm, D_MODEL), lambda i: (i, 0)),
                  pl.BlockSpec((D_MODEL, D_MODEL), lambda i: (0, 0)),
                  pl.BlockSpec((tm, D_MODEL), lambda i: (i, 0)),
                  pl.BlockSpec((1, 1, 3 * D_MODEL), lambda i: (i // tiles_per_seq, 0, 0)),
                  pl.BlockSpec((1, D_MODEL), lambda i: (0, 0))],
        out_specs=pl.BlockSpec((tm, D_MODEL), lambda i: (i, 0)),
        out_shape=jax.ShapeDtypeStruct((m, D_MODEL), F32),
        compiler_params=pltpu.CompilerParams(
            dimension_semantics=("arbitrary",), vmem_limit_bytes=VMEM_LIMIT),
        name="out",
    )(y, wo_bf, x2, mod3, fgain)


def _rope_tables(seq):
    rows = seq // GRID_W
    row = jnp.repeat(jnp.arange(rows, dtype=F32), GRID_W)
    col = jnp.tile(jnp.arange(GRID_W, dtype=F32), rows)
    half = HEAD // 2
    inv_freq = 1.0 / (ROPE_BASE ** (jnp.arange(0, half, 2, dtype=F32) / half))
    ang_r = row[:, None] * inv_freq[None, :]
    ang_c = col[:, None] * inv_freq[None, :]
    ang = jnp.concatenate([ang_r, ang_r, ang_c, ang_c], axis=-1)
    cos, sin = jnp.cos(ang), jnp.sin(ang)
    lo = (jnp.arange(HEAD) % half) < (half // 2)
    return cos, jnp.where(lo, -sin, 0.0), jnp.where(lo, 0.0, sin)


def kernel(x, c, ctx, c_ctx, w_ada, b_ada, norm_gain, w_in, b_merge, lb_logits_fwd,
           lb_logits_bwd, hgrn_norm_gain, w_o_hgrn, sink_logits, w_o_attn, w_out,
           final_norm_gain):
    batch, seq, d = x.shape
    n_ctx = ctx.shape[1]
    assert d == D_MODEL and w_ada.shape[0] == 1 and seq % 1024 == 0 and batch * n_ctx == 1024

    c8 = jnp.concatenate([c, c_ctx[None, :], jnp.zeros((8 - batch - 1, d), F32)], axis=0)
    mod8 = _ada(c8, w_ada[0], b_ada[0][None, :])
    mod3 = mod8[:batch, None, :]
    modc3 = mod8[batch:batch + 1, None, :]

    cos, slo, shi = _rope_tables(seq)
    gain = norm_gain[0][None, :]
    x2 = x.reshape(batch * seq, d)
    w_bf, p_ctx = _wcast_ctx(ctx.reshape(batch * n_ctx, d), modc3, gain, w_in[0])
    p_lat = _inproj_latent(x2, mod3, gain, cos, slo, shi, w_bf, seq)

    att_branch = _attention(p_lat, p_ctx, sink_logits[0][None, :], batch, seq, n_ctx)

    o_back = _hgrn_pass(p_lat, p_ctx, lb_logits_bwd, None, None, batch, seq, n_ctx, reverse=True)
    hg_branch = _hgrn_pass(p_lat, p_ctx, lb_logits_fwd, o_back, hgrn_norm_gain[0][None, :],
                           batch, seq, n_ctx, reverse=False)

    y = _merge(hg_branch, att_branch, w_o_hgrn[0].astype(BF16), w_o_attn[0].astype(BF16),
               p_lat, b_merge[0])
    out = _out(y, w_out[0].astype(BF16), x2, mod3, final_norm_gain[None, :], seq)
    return out.reshape(batch, seq, d)
```

```python
import functools

import jax
import jax.numpy as jnp
import numpy as np
from jax import lax
from jax.experimental import pallas as pl
from jax.experimental.pallas import tpu as pltpu

F32 = jnp.float32
BF16 = jnp.bfloat16

D_MODEL = 2048
GRID_W = 64
EPS = 1e-6
HEAD = 128
ATT_HEADS = 16
ATT_KV_HEADS = 4
ATT_GROUP = ATT_HEADS // ATT_KV_HEADS
WINDOW = 128
ROPE_BASE = 10000.0
LOG2_E = 1.4426950408889634
HG_HEADS = 16
CHUNK = 64
IN_COLS = 19456

P_Q, P_AG, P_HQ, P_HFF, P_HFB, P_HI, P_HG, P_MA, P_MB, P_K, P_V = (
    0, 2048, 4096, 6144, 8192, 10240, 12288, 14336, 16384, 18432, 18944)
C_K, C_V, C_HFF, C_HFB, C_HI, C_COLS = 0, 512, 1024, 3072, 5120, 7168

VMEM_LIMIT = 56 * 1024 * 1024


def _silu(x):
    h = 0.5 * x
    return h + h * jnp.tanh(h)


def _dot_nt(a, b):
    return lax.dot_general(a, b, (((1,), (1,)), ((), ())), preferred_element_type=F32)


def _dot_tn(a, b):
    return lax.dot_general(a, b, (((0,), (0,)), ((), ())), preferred_element_type=F32)


def _ada_kernel(c_ref, w_ref, b_ref, o_ref):
    s = _silu(c_ref[...])
    o_ref[...] = jnp.dot(s.astype(BF16), w_ref[...].astype(BF16),
                         preferred_element_type=F32) + b_ref[...]


def _ada(c8, w_ada, b_ada):
    n_out = w_ada.shape[1]
    tn = 1024
    return pl.pallas_call(
        _ada_kernel,
        grid=(n_out // tn,),
        in_specs=[pl.BlockSpec((8, D_MODEL), lambda j: (0, 0)),
                  pl.BlockSpec((D_MODEL, tn), lambda j: (0, j)),
                  pl.BlockSpec((1, tn), lambda j: (0, j))],
        out_specs=pl.BlockSpec((8, tn), lambda j: (0, j)),
        out_shape=jax.ShapeDtypeStruct((8, n_out), F32),
        compiler_params=pltpu.CompilerParams(
            dimension_semantics=("arbitrary",), vmem_limit_bytes=VMEM_LIMIT),
        name="ada",
    )(c8, w_ada, b_ada)


QUARTER = HEAD // 4


def _pair_lanes(w):
    lane = lax.broadcasted_iota(jnp.int32, (w.shape[0], HEAD), 1)
    second = jnp.logical_and(lane >= QUARTER, lane < 2 * QUARTER)
    third = jnp.logical_and(lane >= 2 * QUARTER, lane < 3 * QUARTER)
    heads = []
    for h in range(w.shape[1] // HEAD):
        t = w[:, h * HEAD:(h + 1) * HEAD]
        up = pltpu.roll(t, HEAD - QUARTER, 1)
        down = pltpu.roll(t, QUARTER, 1)
        heads.append(jnp.where(second, up, jnp.where(third, down, t)))
    return jnp.concatenate(heads, axis=1)


def _rope(t, cos, sin):
    return t * cos + pltpu.roll(t, HEAD // 2, 1) * sin


def _norm_modulate(x_ref, mod_ref, gain_ref):
    x = x_ref[...]
    y = x * lax.rsqrt(jnp.mean(x * x, axis=-1, keepdims=True) + EPS) * gain_ref[...]
    shift = mod_ref[0, :, 0:D_MODEL]
    scale = mod_ref[0, :, D_MODEL:2 * D_MODEL]
    return (y * (1.0 + scale) + shift).astype(BF16)


def _inproj_kernel(x_ref, mod_ref, gain_ref, cos_ref, sin_ref, w_ref, o_ref, xn_ref,
                   *, tn):
    j = pl.program_id(1)

    @pl.when(j == 0)
    def _():
        xn_ref[...] = _norm_modulate(x_ref, mod_ref, gain_ref)

    half = xn_ref.shape[0] // 2
    halves = [slice(0, half), slice(half, 2 * half)]
    accs = [jnp.dot(xn_ref[r, :], w_ref[...], preferred_element_type=F32) for r in halves]
    for r, acc in zip(halves, accs):
        o_ref[r, :] = acc.astype(o_ref.dtype)

    n_heads = tn // HEAD
    q_tiles = P_AG // tn
    kv_tile = P_K // tn
    k_heads = (P_V - P_K) // HEAD
    q_scale = HEAD ** -0.5 * LOG2_E

    def store_rope(n_rot, mult):
        for r, acc in zip(halves, accs):
            for h in range(n_rot):
                sl = slice(h * HEAD, (h + 1) * HEAD)
                rot = _rope(acc[:, sl], cos_ref[r, :], sin_ref[r, :])
                if mult is not None:
                    rot = rot * mult
                o_ref[r, sl] = rot.astype(o_ref.dtype)

    @pl.when(j < q_tiles)
    def _():
        store_rope(n_heads, q_scale)

    @pl.when(j == kv_tile)
    def _():
        store_rope(k_heads, None)


def _inproj_latent(x2, mod3, gain, cos, sin, w_bf, seq):
    m = x2.shape[0]
    tm, tn = 1024, 1024
    tiles_per_seq = seq // tm
    n_tiles = IN_COLS // tn

    def w_map(i, j):
        src = jnp.where(j < 2, j, jnp.where(j == n_tiles - 1, 2, j + 1))
        return (0, src)

    return pl.pallas_call(
        functools.partial(_inproj_kernel, tn=tn),
        grid=(m // tm, n_tiles),
        in_specs=[pl.BlockSpec((tm, D_MODEL), lambda i, j: (i, 0)),
                  pl.BlockSpec((1, 1, 3 * D_MODEL), lambda i, j: (i // tiles_per_seq, 0, 0)),
                  pl.BlockSpec((1, D_MODEL), lambda i, j: (0, 0)),
                  pl.BlockSpec((tm, HEAD), lambda i, j: (i % tiles_per_seq, 0)),
                  pl.BlockSpec((tm, HEAD), lambda i, j: (i % tiles_per_seq, 0)),
                  pl.BlockSpec((D_MODEL, tn), w_map)],
        out_specs=pl.BlockSpec((tm, tn), lambda i, j: (i, j)),
        out_shape=jax.ShapeDtypeStruct((m, IN_COLS), BF16),
        scratch_shapes=[pltpu.VMEM((tm, D_MODEL), BF16)],
        compiler_params=pltpu.CompilerParams(
            dimension_semantics=("arbitrary", "arbitrary"), vmem_limit_bytes=VMEM_LIMIT),
        name="inproj",
    )(x2, mod3, gain, cos, sin, w_bf)


CTX_TN = 512
W_K, W_KV_END, W_FF, W_I_END = 2048, 3072, 7168, 13312
CTX_KV_TILES = (W_K // CTX_TN, W_KV_END // CTX_TN)
CTX_HG_TILES = (W_FF // CTX_TN, W_I_END // CTX_TN)
CTX_HG_SLOT0 = C_HFF // CTX_TN


def _ctx_tile_needed(j):
    return jnp.logical_or(jnp.logical_and(j >= CTX_KV_TILES[0], j < CTX_KV_TILES[1]),
                          jnp.logical_and(j >= CTX_HG_TILES[0], j < CTX_HG_TILES[1]))


def _ctx_out_slot(j):
    kv = jnp.clip(j - CTX_KV_TILES[0], 0, CTX_KV_TILES[1] - CTX_KV_TILES[0] - 1)
    hg = jnp.clip(j - CTX_HG_TILES[0], 0, CTX_HG_TILES[1] - CTX_HG_TILES[0] - 1) + CTX_HG_SLOT0
    return jnp.where(j < CTX_HG_TILES[0], kv, hg)


def _wcast_ctx_kernel(x_ref, mod_ref, gain_ref, w_ref, wb_ref, o_ref, xn_ref):
    j = pl.program_id(0)

    @pl.when(j == 0)
    def _():
        xn_ref[...] = _norm_modulate(x_ref, mod_ref, gain_ref)

    def emit(wb):
        wb_ref[...] = wb

        @pl.when(_ctx_tile_needed(j))
        def _():
            o_ref[...] = jnp.dot(xn_ref[...], wb, preferred_element_type=F32).astype(o_ref.dtype)

    rotary = j < CTX_KV_TILES[0] + (P_V - P_K) // CTX_TN

    @pl.when(rotary)
    def _():
        emit(_pair_lanes(w_ref[...]).astype(BF16))

    @pl.when(jnp.logical_not(rotary))
    def _():
        emit(w_ref[...].astype(BF16))


def _wcast_ctx(ctx2, modc3, gain, w_in):
    m = ctx2.shape[0]
    tn = CTX_TN
    return pl.pallas_call(
        _wcast_ctx_kernel,
        grid=(IN_COLS // tn,),
        in_specs=[pl.BlockSpec((m, D_MODEL), lambda j: (0, 0)),
                  pl.BlockSpec((1, 1, 3 * D_MODEL), lambda j: (0, 0, 0)),
                  pl.BlockSpec((1, D_MODEL), lambda j: (0, 0)),
                  pl.BlockSpec((D_MODEL, tn), lambda j: (0, j))],
        out_specs=[pl.BlockSpec((D_MODEL, tn), lambda j: (0, j)),
                   pl.BlockSpec((m, tn), lambda j: (0, _ctx_out_slot(j)))],
        out_shape=[jax.ShapeDtypeStruct((D_MODEL, IN_COLS), BF16),
                   jax.ShapeDtypeStruct((m, C_COLS), BF16)],
        scratch_shapes=[pltpu.VMEM((m, D_MODEL), BF16)],
        compiler_params=pltpu.CompilerParams(
            dimension_semantics=("arbitrary",), vmem_limit_bytes=VMEM_LIMIT),
        name="wcast_ctx",
    )(ctx2, modc3, gain, w_in)


def _attn_kernel(sink_ref, q_ref, ag_ref, kp_ref, kc_ref, kn_ref, vp_ref, vc_ref, vn_ref,
                 kx_ref, vx_ref, o_ref, *, n_blocks):
    i = pl.program_id(1)
    qb = q_ref.shape[0]
    row = lax.broadcasted_iota(jnp.int32, (ATT_GROUP * qb, qb), 0) & (qb - 1)
    col = lax.broadcasted_iota(jnp.int32, (ATT_GROUP * qb, qb), 1)
    ok_prev = col >= row + jnp.where(i > 0, 0, 2 * qb)
    ok_next = col + jnp.where(i < n_blocks - 1, 0, 2 * qb) <= row

    def scores(kh):
        ks = slice(kh * HEAD, (kh + 1) * HEAD)
        kcat = jnp.concatenate([kp_ref[:, ks], kc_ref[:, ks], kn_ref[:, ks], kx_ref[:, ks]], axis=0)
        q4 = jnp.concatenate(
            [q_ref[:, (kh * ATT_GROUP + g) * HEAD:(kh * ATT_GROUP + g + 1) * HEAD]
             for g in range(ATT_GROUP)], axis=0)
        return _dot_nt(q4, kcat)

    def finish(kh, s):
        ks = slice(kh * HEAD, (kh + 1) * HEAD)
        vcat = jnp.concatenate([vp_ref[:, ks], vc_ref[:, ks], vn_ref[:, ks], vx_ref[:, ks]], axis=0)
        heads = [kh * ATT_GROUP + g for g in range(ATT_GROUP)]
        s = jnp.concatenate(
            [jnp.where(ok_prev, s[:, 0:qb], -jnp.inf), s[:, qb:2 * qb],
             jnp.where(ok_next, s[:, 2 * qb:3 * qb], -jnp.inf), s[:, 3 * qb:]], axis=1)
        sink = jnp.concatenate(
            [jnp.full((qb, 1), sink_ref[0, h] * LOG2_E, F32) for h in heads], axis=0)
        m = jnp.maximum(jnp.max(s, axis=-1, keepdims=True), sink)
        p = jnp.exp2(s - m)
        den = jnp.sum(p, axis=-1, keepdims=True) + jnp.exp2(sink - m)
        o4 = jnp.dot(p.astype(BF16), vcat, preferred_element_type=F32) / den
        for g, h in enumerate(heads):
            hs = slice(h * HEAD, (h + 1) * HEAD)
            gate = _silu(ag_ref[:, hs].astype(F32))
            o_ref[:, hs] = (o4[g * qb:(g + 1) * qb, :] * gate).astype(o_ref.dtype)

    pending = scores(0)
    for kh in range(ATT_KV_HEADS):
        upcoming = scores(kh + 1) if kh + 1 < ATT_KV_HEADS else None
        finish(kh, pending)
        pending = upcoming


def _attention(p_lat, p_ctx, sink, batch, seq, n_ctx):
    qb = WINDOW
    nb = seq // qb
    kvw = ATT_KV_HEADS * HEAD
    kcol, vcol = P_K // kvw, P_V // kvw

    def rows(off):
        return lambda b, i: b * nb + jnp.clip(i + off, 0, nb - 1)

    def spec(w, rfn, c):
        return pl.BlockSpec((qb, w), lambda b, i: (rfn(b, i), c))

    return pl.pallas_call(
        functools.partial(_attn_kernel, n_blocks=nb),
        grid=(batch, nb),
        in_specs=[pl.BlockSpec(memory_space=pltpu.SMEM),
                  spec(D_MODEL, rows(0), P_Q // D_MODEL),
                  spec(D_MODEL, rows(0), P_AG // D_MODEL),
                  spec(kvw, rows(-1), kcol), spec(kvw, rows(0), kcol), spec(kvw, rows(1), kcol),
                  spec(kvw, rows(-1), vcol), spec(kvw, rows(0), vcol), spec(kvw, rows(1), vcol),
                  pl.BlockSpec((n_ctx, kvw), lambda b, i: (b, C_K // kvw)),
                  pl.BlockSpec((n_ctx, kvw), lambda b, i: (b, C_V // kvw))],
        out_specs=pl.BlockSpec((qb, D_MODEL), lambda b, i: (b * nb + i, 0)),
        out_shape=jax.ShapeDtypeStruct((batch * seq, D_MODEL), BF16),
        compiler_params=pltpu.CompilerParams(
            dimension_semantics=("arbitrary", "arbitrary"), vmem_limit_bytes=VMEM_LIMIT),
        name="attn",
    )(sink, p_lat, p_lat, p_lat, p_lat, p_lat, p_lat, p_lat, p_lat, p_ctx, p_ctx)


SUB = 16
N_SUB = CHUNK // SUB
ROW_FACTORS = 24


def _hgrn_decay_matrix(reverse):
    m = np.zeros((2 * CHUNK + 32, CHUNK), np.float32)
    r = 2 * CHUNK
    for i in range(N_SUB):
        lo, hi, mid = i * SUB, (i + 1) * SUB, i * SUB + SUB // 2
        for t in range(lo, hi):
            if reverse:
                m[t, t:hi] = 1
                m[CHUNK + t, lo:t] = 1
            else:
                m[t, lo:t + 1] = 1
                m[CHUNK + t, t + 1:hi] = 1
        if reverse:
            before, after, first, second = slice(hi, CHUNK), slice(0, lo), slice(mid, hi), slice(lo, mid)
        else:
            before, after, first, second = slice(0, lo), slice(hi, CHUNK), slice(lo, mid), slice(mid, hi)
        m[r + i, before] = 1
        m[r + 4 + i, after] = 1
        m[r + 8 + i, first] = -1
        m[r + 12 + i, second] = -1
    b2, b1 = (1, 2) if reverse else (2, 1)
    m[r + 16, b2 * SUB:(b2 + 1) * SUB] = 1
    m[r + 17, b1 * SUB:(b1 + 1) * SUB] = 1
    m[r + 18, :] = 1
    return np.concatenate([m, m], axis=1)


def _cumsum_rows(tri_bf, g):
    g1 = g.astype(BF16)
    r1 = g - g1.astype(F32)
    g2 = r1.astype(BF16)
    g3 = (r1 - g2.astype(F32)).astype(BF16)
    dot = functools.partial(jnp.dot, preferred_element_type=F32)
    return dot(tri_bf, g1) + dot(tri_bf, g2) + dot(tri_bf, g3)


def _hgrn_kernel(lbl_ref, dm_ref, cz_ref, cv_ref, q_ref, z_ref, v_ref, *rest, reverse, finalize,
                 heads, chunks_per_step):
    if finalize:
        hg_ref, ob_ref, gain_ref, o_ref, st_ref = rest
    else:
        o_ref, st_ref = rest
    t = pl.program_id(2)
    c = CHUNK
    row = lax.broadcasted_iota(jnp.int32, (c, c), 0)
    col = lax.broadcasted_iota(jnp.int32, (c, c), 1)
    tri = (col >= row) if reverse else (col <= row)
    end = 0 if reverse else c - 1

    l0, l1 = lbl_ref[0:1, :], lbl_ref[1:2, :]
    lm = jnp.maximum(l0, l1)
    e0, e1 = jnp.exp(l0 - lm), jnp.exp(l1 - lm)
    lb = e0 / (e0 + e1)
    fa, fb = lb + 0.5 * (1.0 - lb), 0.5 * (1.0 - lb)

    def forget(z):
        f = fa + fb * jnp.tanh(0.5 * z.astype(F32))
        return jnp.log(f), 1.0 - f

    @pl.when(t == 0)
    def _():
        tri_bf = jnp.where(tri, 1.0, 0.0).astype(BF16)
        n_cc = cz_ref.shape[0] // c
        s_ctx = [jnp.zeros((HEAD, HEAD), F32) for _ in range(heads)]
        for ci in (range(n_cc - 1, -1, -1) if reverse else range(n_cc)):
            rs = slice(ci * c, (ci + 1) * c)
            g, kk = forget(cz_ref[rs, :])
            gc = _cumsum_rows(tri_bf, g)
            for h in range(heads):
                hs = slice(h * HEAD, (h + 1) * HEAD)
                gend = gc[end:end + 1, hs]
                ks = (kk[:, hs] * jnp.exp(gend - gc[:, hs])).astype(BF16)
                s_ctx[h] = jnp.exp(gend) * s_ctx[h] + _dot_tn(cv_ref[rs, hs], ks)
        for h in range(heads):
            st_ref[h] = s_ctx[h]

    def blk(x, i):
        return x[i * SUB:(i + 1) * SUB]

    def tb(tau):
        return N_SUB - 1 - tau if reverse else tau

    zero = jnp.zeros((SUB, HEAD), BF16)

    def place(parts):
        return jnp.concatenate([parts.get(i, zero) for i in range(N_SUB)], axis=0)

    diag = jnp.logical_and(tri, (row ^ col) < SUB)

    def scores(ci):
        rs = slice(ci * c, (ci + 1) * c)
        g, kk = forget(z_ref[rs, :])
        g1 = g.astype(BF16)
        g2 = (g - g1.astype(F32)).astype(BF16)
        ld = jnp.dot(dm_ref[...], jnp.concatenate([g1, g2], axis=0), preferred_element_type=F32)
        hq = 0.5 * q_ref[rs, :].astype(F32)
        q_hat = (hq + hq * jnp.tanh(hq)) * jnp.exp(ld[0:c])
        k_hat = kk * jnp.exp(ld[c:2 * c])
        rf = jnp.exp(ld[2 * c:2 * c + ROW_FACTORS])
        per_head = []
        for h in range(heads):
            hs = slice(h * HEAD, (h + 1) * HEAD)
            q_h, k_h = q_hat[:, hs], k_hat[:, hs]

            def scaled(x, r0):
                return jnp.concatenate(
                    [blk(x, i) * rf[r0 + i:r0 + i + 1, hs] for i in range(N_SUB)],
                    axis=0).astype(BF16)

            qi, ks, qd, kd = scaled(q_h, 0), scaled(k_h, 4), scaled(q_h, 8), scaled(k_h, 12)
            qb, kb = q_h.astype(BF16), k_h.astype(BF16)
            q3 = (blk(q_h, tb(3)) * rf[16:17, hs]).astype(BF16)
            k0 = (blk(k_h, tb(0)) * rf[17:18, hs]).astype(BF16)
            ql = jnp.concatenate(
                [place({tb(1): blk(qb, tb(1))}), place({tb(3): blk(qb, tb(3))}),
                 place({tb(2): blk(qb, tb(2)), tb(3): q3})], axis=1)
            kl = jnp.concatenate(
                [place({tb(0): blk(kb, tb(0))}), place({tb(2): blk(kb, tb(2))}),
                 place({tb(1): blk(kb, tb(1)), tb(0): k0})], axis=1)
            per_head.append((qi, ks, _dot_nt(ql, kl), _dot_nt(qd, kd), rf[18:19, hs]))
        return per_head

    def outputs(ci, per_head, state):
        rs = slice(ci * c, (ci + 1) * c)
        a_bf = [jnp.where(diag, a_dg, a_off).astype(BF16) for _, _, a_off, a_dg, _ in per_head]
        outs, new_state = [], []
        for h in range(heads):
            hs = slice(h * HEAD, (h + 1) * HEAD)
            qi, ks, _, _, dec = per_head[h]
            v_h = v_ref[rs, hs]
            outs.append(_dot_nt(qi, state[h].astype(BF16))
                        + jnp.dot(a_bf[h], v_h, preferred_element_type=F32))
            new_state.append(dec * state[h] + _dot_tn(v_h, ks))
        for h in range(heads):
            hs = slice(h * HEAD, (h + 1) * HEAD)
            if finalize:
                ot = outs[h] + ob_ref[rs, hs]
                y = ot * lax.rsqrt(jnp.mean(ot * ot, axis=-1, keepdims=True) + EPS)
                y = y * gain_ref[:, hs]
                o_ref[rs, hs] = (y * _silu(hg_ref[rs, hs].astype(F32))).astype(o_ref.dtype)
            else:
                o_ref[rs, hs] = outs[h]
        return new_state

    state = [st_ref[h] for h in range(heads)]
    order = list(range(chunks_per_step - 1, -1, -1) if reverse else range(chunks_per_step))
    pending = scores(order[0])
    for n, ci in enumerate(order):
        upcoming = scores(order[n + 1]) if n + 1 < len(order) else None
        state = outputs(ci, pending, state)
        pending = upcoming
    for h in range(heads):
        st_ref[h] = state[h]


def _hgrn_pass(p_lat, p_ctx, lb_logits, o_back, gain, batch, seq, n_ctx, *, reverse):
    finalize = o_back is not None
    heads = 8
    hw = heads * HEAD
    cps = 8
    tb = cps * CHUNK
    steps = seq // tb
    hblocks = HG_HEADS // heads
    dm = jnp.asarray(_hgrn_decay_matrix(reverse), BF16)

    def rblk(b, t):
        return b * steps + ((steps - 1 - t) if reverse else t)

    def lat(col0):
        return pl.BlockSpec((tb, hw), lambda b, h, t: (rblk(b, t), col0 // hw + h))

    z_lat, z_ctx = (P_HFB, C_HFB) if reverse else (P_HFF, C_HFF)
    in_specs = [pl.BlockSpec((2, hw), lambda b, h, t: (0, h)),
                pl.BlockSpec(dm.shape, lambda b, h, t: (0, 0)),
                pl.BlockSpec((n_ctx, hw), lambda b, h, t: (b, z_ctx // hw + h)),
                pl.BlockSpec((n_ctx, hw), lambda b, h, t: (b, C_HI // hw + h)),
                lat(P_HQ), lat(z_lat), lat(P_HI)]
    args = [lb_logits, dm, p_ctx, p_ctx, p_lat, p_lat, p_lat]
    if finalize:
        in_specs += [lat(P_HG), lat(0), pl.BlockSpec((1, hw), lambda b, h, t: (0, h))]
        args += [p_lat, o_back, gain]
    return pl.pallas_call(
        functools.partial(_hgrn_kernel, reverse=reverse, finalize=finalize, heads=heads,
                          chunks_per_step=cps),
        grid=(batch, hblocks, steps),
        in_specs=in_specs,
        out_specs=lat(0),
        out_shape=jax.ShapeDtypeStruct((batch * seq, HG_HEADS * HEAD), BF16 if finalize else F32),
        scratch_shapes=[pltpu.VMEM((heads, HEAD, HEAD), F32)],
        compiler_params=pltpu.CompilerParams(
            dimension_semantics=("arbitrary", "arbitrary", "arbitrary"),
            vmem_limit_bytes=VMEM_LIMIT),
        name="hgrn_fwd" if finalize else "hgrn_bwd",
    )(*args)


def _merge_kernel(hg_ref, at_ref, wh_ref, wa_ref, ma_ref, mb_ref, bm_ref, o_ref):
    half = hg_ref.shape[0] // 2
    halves = [slice(0, half), slice(half, 2 * half)]
    prods = [(jnp.dot(hg_ref[r, :], wh_ref[...], preferred_element_type=F32),
              jnp.dot(at_ref[r, :], wa_ref[...], preferred_element_type=F32)) for r in halves]
    for r, (ya, yb) in zip(halves, prods):
        ga = jax.nn.sigmoid(ma_ref[r, :].astype(F32) + bm_ref[0:1, :])
        gb = jax.nn.sigmoid(mb_ref[r, :].astype(F32) + bm_ref[1:2, :])
        o_ref[r, :] = (ga * ya + gb * yb).astype(o_ref.dtype)


def _merge(hg_branch, att_branch, wh_bf, wa_bf, p_lat, b_merge):
    m = hg_branch.shape[0]
    tm, tn = 1024, 512
    return pl.pallas_call(
        _merge_kernel,
        grid=(m // tm, D_MODEL // tn),
        in_specs=[pl.BlockSpec((tm, D_MODEL), lambda i, j: (i, 0)),
                  pl.BlockSpec((tm, D_MODEL), lambda i, j: (i, 0)),
                  pl.BlockSpec((D_MODEL, tn), lambda i, j: (0, j)),
                  pl.BlockSpec((D_MODEL, tn), lambda i, j: (0, j)),
                  pl.BlockSpec((tm, tn), lambda i, j: (i, P_MA // tn + j)),
                  pl.BlockSpec((tm, tn), lambda i, j: (i, P_MB // tn + j)),
                  pl.BlockSpec((2, tn), lambda i, j: (0, j))],
        out_specs=pl.BlockSpec((tm, tn), lambda i, j: (i, j)),
        out_shape=jax.ShapeDtypeStruct((m, D_MODEL), BF16),
        compiler_params=pltpu.CompilerParams(
            dimension_semantics=("arbitrary", "arbitrary"), vmem_limit_bytes=VMEM_LIMIT),
        name="merge",
    )(hg_branch, att_branch, wh_bf, wa_bf, p_lat, p_lat, b_merge)


def _out_kernel(y_ref, w_ref, x_ref, mod_ref, gain_ref, o_ref):
    half = y_ref.shape[0] // 2
    halves = [slice(0, half), slice(half, 2 * half)]
    zs = [jnp.dot(y_ref[r, :], w_ref[...], preferred_element_type=F32) for r in halves]
    gate = mod_ref[0, :, 2 * D_MODEL:3 * D_MODEL]
    for r, z in zip(halves, zs):
        h = x_ref[r, :] + gate * z
        y = h * lax.rsqrt(jnp.mean(h * h, axis=-1, keepdims=True) + EPS)
        o_ref[r, :] = y * gain_ref[...]


def _out(y, wo_bf, x2, mod3, fgain, seq):
    m = y.shape[0]
    tm = 512
    tiles_per_seq = seq // tm
    return pl.pallas_call(
        _out_kernel,
        grid=(m // tm,),
        in_specs=[pl.BlockSpec((tm, D_MODEL), lambda i: (i, 0)),
                  pl.BlockSpec((D_MODEL, D_MODEL), lambda i: (0, 0)),
                  pl.BlockSpec((tm, D_MODEL), lambda i: (i, 0)),
                  pl.BlockSpec((1, 1, 3 * D_MODEL), lambda i: (i // tiles_per_seq, 0, 0)),
                  pl.BlockSpec((1, D_MODEL), lambda i: (0, 0))],
        out_specs=pl.BlockSpec((tm, D_MODEL), lambda i: (i, 0)),
        out_shape=jax.ShapeDtypeStruct((m, D_MODEL), F32),
        compiler_params=pltpu.CompilerParams(
            dimension_semantics=("arbitrary",), vmem_limit_bytes=VMEM_LIMIT),
        name="out",
    )(y, wo_bf, x2, mod3, fgain)


def _rope_tables(seq):
    rows = seq // GRID_W
    row = jnp.repeat(jnp.arange(rows, dtype=F32), GRID_W)
    col = jnp.tile(jnp.arange(GRID_W, dtype=F32), rows)
    half = HEAD // 2
    inv_freq = 1.0 / (ROPE_BASE ** (jnp.arange(0, half, 2, dtype=F32) / half))
    ang_r = row[:, None] * inv_freq[None, :]
    ang_c = col[:, None] * inv_freq[None, :]
    ang = jnp.concatenate([ang_r, ang_c, ang_r, ang_c], axis=-1)
    sign = jnp.where(jnp.arange(HEAD) < half, -1.0, 1.0)
    return jnp.cos(ang), jnp.sin(ang) * sign


def kernel(x, c, ctx, c_ctx, w_ada, b_ada, norm_gain, w_in, b_merge, lb_logits_fwd,
           lb_logits_bwd, hgrn_norm_gain, w_o_hgrn, sink_logits, w_o_attn, w_out,
           final_norm_gain):
    batch, seq, d = x.shape
    n_ctx = ctx.shape[1]
    assert d == D_MODEL and w_ada.shape[0] == 1 and seq % 1024 == 0 and batch * n_ctx == 1024

    c8 = jnp.concatenate([c, c_ctx[None, :], jnp.zeros((8 - batch - 1, d), F32)], axis=0)
    mod8 = _ada(c8, w_ada[0], b_ada[0][None, :])
    mod3 = mod8[:batch, None, :]
    modc3 = mod8[batch:batch + 1, None, :]

    cos, sin = _rope_tables(seq)
    gain = norm_gain[0][None, :]
    x2 = x.reshape(batch * seq, d)
    w_bf, p_ctx = _wcast_ctx(ctx.reshape(batch * n_ctx, d), modc3, gain, w_in[0])
    p_lat = _inproj_latent(x2, mod3, gain, cos, sin, w_bf, seq)

    att_branch = _attention(p_lat, p_ctx, sink_logits[0][None, :], batch, seq, n_ctx)

    o_back = _hgrn_pass(p_lat, p_ctx, lb_logits_bwd, None, None, batch, seq, n_ctx, reverse=True)
    hg_branch = _hgrn_pass(p_lat, p_ctx, lb_logits_fwd, o_back, hgrn_norm_gain[0][None, :],
                           batch, seq, n_ctx, reverse=False)

    y = _merge(hg_branch, att_branch, w_o_hgrn[0].astype(BF16), w_o_attn[0].astype(BF16),
               p_lat, b_merge[0])
    out = _out(y, w_out[0].astype(BF16), x2, mod3, final_norm_gain[None, :], seq)
    return out.reshape(batch, seq, d)
```

```python
import functools

import jax
import jax.numpy as jnp
import numpy as np
from jax import lax
from jax.experimental import pallas as pl
from jax.experimental.pallas import tpu as pltpu

F32 = jnp.float32
BF16 = jnp.bfloat16

D_MODEL = 2048
GRID_W = 64
EPS = 1e-6
HEAD = 128
ATT_HEADS = 16
ATT_KV_HEADS = 4
ATT_GROUP = ATT_HEADS // ATT_KV_HEADS
WINDOW = 128
ROPE_BASE = 10000.0
LOG2_E = 1.4426950408889634
HG_HEADS = 16
CHUNK = 64
IN_COLS = 19456

P_Q, P_AG, P_HQ, P_HFF, P_HFB, P_HI, P_HG, P_MA, P_MB, P_K, P_V = (
    0, 2048, 4096, 6144, 8192, 10240, 12288, 14336, 16384, 18432, 18944)
C_K, C_V, C_HFF, C_HFB, C_HI, C_COLS = 0, 512, 1024, 3072, 5120, 7168

VMEM_LIMIT = 56 * 1024 * 1024


def _silu(x):
    h = 0.5 * x
    return h + h * jnp.tanh(h)


def _dot_nt(a, b):
    return lax.dot_general(a, b, (((1,), (1,)), ((), ())), preferred_element_type=F32)


def _dot_tn(a, b):
    return lax.dot_general(a, b, (((0,), (0,)), ((), ())), preferred_element_type=F32)


def _ada_kernel(c_ref, w_ref, b_ref, o_ref):
    s = _silu(c_ref[...])
    o_ref[...] = jnp.dot(s.astype(BF16), w_ref[...].astype(BF16),
                         preferred_element_type=F32) + b_ref[...]


def _ada(c8, w_ada, b_ada):
    n_out = w_ada.shape[1]
    tn = 1024
    return pl.pallas_call(
        _ada_kernel,
        grid=(n_out // tn,),
        in_specs=[pl.BlockSpec((8, D_MODEL), lambda j: (0, 0)),
                  pl.BlockSpec((D_MODEL, tn), lambda j: (0, j)),
                  pl.BlockSpec((1, tn), lambda j: (0, j))],
        out_specs=pl.BlockSpec((8, tn), lambda j: (0, j)),
        out_shape=jax.ShapeDtypeStruct((8, n_out), F32),
        compiler_params=pltpu.CompilerParams(
            dimension_semantics=("arbitrary",), vmem_limit_bytes=VMEM_LIMIT),
        name="ada",
    )(c8, w_ada, b_ada)


QUARTER = HEAD // 4


def _pair_lanes(w):
    lane = lax.broadcasted_iota(jnp.int32, (w.shape[0], HEAD), 1)
    second = jnp.logical_and(lane >= QUARTER, lane < 2 * QUARTER)
    third = jnp.logical_and(lane >= 2 * QUARTER, lane < 3 * QUARTER)
    heads = []
    for h in range(w.shape[1] // HEAD):
        t = w[:, h * HEAD:(h + 1) * HEAD]
        up = pltpu.roll(t, HEAD - QUARTER, 1)
        down = pltpu.roll(t, QUARTER, 1)
        heads.append(jnp.where(second, up, jnp.where(third, down, t)))
    return jnp.concatenate(heads, axis=1)


def _rope(t, cos, sin):
    return t * cos + pltpu.roll(t, HEAD // 2, 1) * sin


def _norm_modulate(x_ref, mod_ref, gain_ref):
    x = x_ref[...]
    y = x * lax.rsqrt(jnp.mean(x * x, axis=-1, keepdims=True) + EPS) * gain_ref[...]
    shift = mod_ref[0, :, 0:D_MODEL]
    scale = mod_ref[0, :, D_MODEL:2 * D_MODEL]
    return (y * (1.0 + scale) + shift).astype(BF16)


def _inproj_kernel(x_ref, mod_ref, gain_ref, cos_ref, sin_ref, w_ref, o_ref, xn_ref,
                   *, tn):
    j = pl.program_id(1)

    @pl.when(j == 0)
    def _():
        xn_ref[...] = _norm_modulate(x_ref, mod_ref, gain_ref)

    half = xn_ref.shape[0] // 2
    halves = [slice(0, half), slice(half, 2 * half)]
    accs = [jnp.dot(xn_ref[r, :], w_ref[...], preferred_element_type=F32) for r in halves]
    for r, acc in zip(halves, accs):
        o_ref[r, :] = acc.astype(o_ref.dtype)

    n_heads = tn // HEAD
    q_tiles = P_AG // tn
    kv_tile = P_K // tn
    k_heads = (P_V - P_K) // HEAD
    q_scale = HEAD ** -0.5 * LOG2_E

    def store_rope(n_rot, mult):
        for r, acc in zip(halves, accs):
            for h in range(n_rot):
                sl = slice(h * HEAD, (h + 1) * HEAD)
                rot = _rope(acc[:, sl], cos_ref[r, :], sin_ref[r, :])
                if mult is not None:
                    rot = rot * mult
                o_ref[r, sl] = rot.astype(o_ref.dtype)

    @pl.when(j < q_tiles)
    def _():
        store_rope(n_heads, q_scale)

    @pl.when(j == kv_tile)
    def _():
        store_rope(k_heads, None)


def _inproj_latent(x2, mod3, gain, cos, sin, w_bf, seq):
    m = x2.shape[0]
    tm, tn = 1024, 1024
    tiles_per_seq = seq // tm
    n_tiles = IN_COLS // tn

    def w_map(i, j):
        src = jnp.where(j < 2, j, jnp.where(j == n_tiles - 1, 2, j + 1))
        return (0, src)

    return pl.pallas_call(
        functools.partial(_inproj_kernel, tn=tn),
        grid=(m // tm, n_tiles),
        in_specs=[pl.BlockSpec((tm, D_MODEL), lambda i, j: (i, 0)),
                  pl.BlockSpec((1, 1, 3 * D_MODEL), lambda i, j: (i // tiles_per_seq, 0, 0)),
                  pl.BlockSpec((1, D_MODEL), lambda i, j: (0, 0)),
                  pl.BlockSpec((tm, HEAD), lambda i, j: (i % tiles_per_seq, 0)),
                  pl.BlockSpec((tm, HEAD), lambda i, j: (i % tiles_per_seq, 0)),
                  pl.BlockSpec((D_MODEL, tn), w_map)],
        out_specs=pl.BlockSpec((tm, tn), lambda i, j: (i, j)),
        out_shape=jax.ShapeDtypeStruct((m, IN_COLS), BF16),
        scratch_shapes=[pltpu.VMEM((tm, D_MODEL), BF16)],
        compiler_params=pltpu.CompilerParams(
            dimension_semantics=("arbitrary", "arbitrary"), vmem_limit_bytes=VMEM_LIMIT),
        name="inproj",
    )(x2, mod3, gain, cos, sin, w_bf)


CTX_TN = 512
W_K, W_KV_END, W_FF, W_I_END = 2048, 3072, 7168, 13312
CTX_KV_TILES = (W_K // CTX_TN, W_KV_END // CTX_TN)
CTX_HG_TILES = (W_FF // CTX_TN, W_I_END // CTX_TN)
CTX_HG_SLOT0 = C_HFF // CTX_TN


def _ctx_tile_needed(j):
    return jnp.logical_or(jnp.logical_and(j >= CTX_KV_TILES[0], j < CTX_KV_TILES[1]),
                          jnp.logical_and(j >= CTX_HG_TILES[0], j < CTX_HG_TILES[1]))


def _ctx_out_slot(j):
    kv = jnp.clip(j - CTX_KV_TILES[0], 0, CTX_KV_TILES[1] - CTX_KV_TILES[0] - 1)
    hg = jnp.clip(j - CTX_HG_TILES[0], 0, CTX_HG_TILES[1] - CTX_HG_TILES[0] - 1) + CTX_HG_SLOT0
    return jnp.where(j < CTX_HG_TILES[0], kv, hg)


def _wcast_ctx_kernel(x_ref, mod_ref, gain_ref, w_ref, wb_ref, o_ref, xn_ref):
    j = pl.program_id(0)

    @pl.when(j == 0)
    def _():
        xn_ref[...] = _norm_modulate(x_ref, mod_ref, gain_ref)

    def emit(wb):
        wb_ref[...] = wb

        @pl.when(_ctx_tile_needed(j))
        def _():
            o_ref[...] = jnp.dot(xn_ref[...], wb, preferred_element_type=F32).astype(o_ref.dtype)

    rotary = j < CTX_KV_TILES[0] + (P_V - P_K) // CTX_TN

    @pl.when(rotary)
    def _():
        emit(_pair_lanes(w_ref[...]).astype(BF16))

    @pl.when(jnp.logical_not(rotary))
    def _():
        emit(w_ref[...].astype(BF16))


def _wcast_ctx(ctx2, modc3, gain, w_in):
    m = ctx2.shape[0]
    tn = CTX_TN
    return pl.pallas_call(
        _wcast_ctx_kernel,
        grid=(IN_COLS // tn,),
        in_specs=[pl.BlockSpec((m, D_MODEL), lambda j: (0, 0)),
                  pl.BlockSpec((1, 1, 3 * D_MODEL), lambda j: (0, 0, 0)),
                  pl.BlockSpec((1, D_MODEL), lambda j: (0, 0)),
                  pl.BlockSpec((D_MODEL, tn), lambda j: (0, j))],
        out_specs=[pl.BlockSpec((D_MODEL, tn), lambda j: (0, j)),
                   pl.BlockSpec((m, tn), lambda j: (0, _ctx_out_slot(j)))],
        out_shape=[jax.ShapeDtypeStruct((D_MODEL, IN_COLS), BF16),
                   jax.ShapeDtypeStruct((m, C_COLS), BF16)],
        scratch_shapes=[pltpu.VMEM((m, D_MODEL), BF16)],
        compiler_params=pltpu.CompilerParams(
            dimension_semantics=("arbitrary",), vmem_limit_bytes=VMEM_LIMIT),
        name="wcast_ctx",
    )(ctx2, modc3, gain, w_in)


ATT_QBLOCKS = 2


def _attn_kernel(sink_ref, q_ref, ag_ref, *rest, n_steps):
    n_kv = ATT_QBLOCKS + 2
    k_refs, v_refs = rest[:n_kv], rest[n_kv:2 * n_kv]
    kx_ref, vx_ref, o_ref = rest[2 * n_kv:]
    i = pl.program_id(1)
    qb = WINDOW
    row = lax.broadcasted_iota(jnp.int32, (ATT_GROUP * qb, qb), 0) & (qb - 1)
    col = lax.broadcasted_iota(jnp.int32, (ATT_GROUP * qb, qb), 1)
    ok_prev = [col >= row + (jnp.where(i > 0, 0, 2 * qb) if u == 0 else 0)
               for u in range(ATT_QBLOCKS)]
    ok_next = [col + (jnp.where(i < n_steps - 1, 0, 2 * qb) if u == ATT_QBLOCKS - 1 else 0) <= row
               for u in range(ATT_QBLOCKS)]

    def scores(u, kh):
        ks = slice(kh * HEAD, (kh + 1) * HEAD)
        rs = slice(u * qb, (u + 1) * qb)
        kcat = jnp.concatenate([k_refs[u][:, ks], k_refs[u + 1][:, ks], k_refs[u + 2][:, ks],
                                kx_ref[:, ks]], axis=0)
        q4 = jnp.concatenate(
            [q_ref[rs, (kh * ATT_GROUP + g) * HEAD:(kh * ATT_GROUP + g + 1) * HEAD]
             for g in range(ATT_GROUP)], axis=0)
        return _dot_nt(q4, kcat)

    def finish(u, kh, s):
        ks = slice(kh * HEAD, (kh + 1) * HEAD)
        rs = slice(u * qb, (u + 1) * qb)
        vcat = jnp.concatenate([v_refs[u][:, ks], v_refs[u + 1][:, ks], v_refs[u + 2][:, ks],
                                vx_ref[:, ks]], axis=0)
        heads = [kh * ATT_GROUP + g for g in range(ATT_GROUP)]
        s = jnp.concatenate(
            [jnp.where(ok_prev[u], s[:, 0:qb], -jnp.inf), s[:, qb:2 * qb],
             jnp.where(ok_next[u], s[:, 2 * qb:3 * qb], -jnp.inf), s[:, 3 * qb:]], axis=1)
        sink = jnp.concatenate(
            [jnp.full((qb, 1), sink_ref[0, h] * LOG2_E, F32) for h in heads], axis=0)
        m = jnp.maximum(jnp.max(s, axis=-1, keepdims=True), sink)
        p = jnp.exp2(s - m)
        den = jnp.sum(p, axis=-1, keepdims=True) + jnp.exp2(sink - m)
        o4 = jnp.dot(p.astype(BF16), vcat, preferred_element_type=F32) / den
        for g, h in enumerate(heads):
            hs = slice(h * HEAD, (h + 1) * HEAD)
            gate = _silu(ag_ref[rs, hs].astype(F32))
            o_ref[rs, hs] = (o4[g * qb:(g + 1) * qb, :] * gate).astype(o_ref.dtype)

    work = [(u, kh) for u in range(ATT_QBLOCKS) for kh in range(ATT_KV_HEADS)]
    pending = scores(*work[0])
    for n, (u, kh) in enumerate(work):
        upcoming = scores(*work[n + 1]) if n + 1 < len(work) else None
        finish(u, kh, pending)
        pending = upcoming


def _attention(p_lat, p_ctx, sink, batch, seq, n_ctx):
    qb = WINDOW
    nb = seq // qb
    steps = nb // ATT_QBLOCKS
    kvw = ATT_KV_HEADS * HEAD
    kcol, vcol = P_K // kvw, P_V // kvw

    def kv_spec(off, c):
        return pl.BlockSpec(
            (qb, kvw), lambda b, i: (b * nb + jnp.clip(i * ATT_QBLOCKS + off, 0, nb - 1), c))

    def q_spec(c):
        return pl.BlockSpec((ATT_QBLOCKS * qb, D_MODEL), lambda b, i: (b * steps + i, c))

    offs = range(-1, ATT_QBLOCKS + 1)
    return pl.pallas_call(
        functools.partial(_attn_kernel, n_steps=steps),
        grid=(batch, steps),
        in_specs=[pl.BlockSpec(memory_space=pltpu.SMEM),
                  q_spec(P_Q // D_MODEL), q_spec(P_AG // D_MODEL)]
        + [kv_spec(off, kcol) for off in offs] + [kv_spec(off, vcol) for off in offs]
        + [pl.BlockSpec((n_ctx, kvw), lambda b, i: (b, C_K // kvw)),
           pl.BlockSpec((n_ctx, kvw), lambda b, i: (b, C_V // kvw))],
        out_specs=q_spec(0),
        out_shape=jax.ShapeDtypeStruct((batch * seq, D_MODEL), BF16),
        compiler_params=pltpu.CompilerParams(
            dimension_semantics=("arbitrary", "arbitrary"), vmem_limit_bytes=VMEM_LIMIT),
        name="attn",
    )(sink, p_lat, p_lat, *([p_lat] * (2 * (ATT_QBLOCKS + 2))), p_ctx, p_ctx)


SUB = 16
N_SUB = CHUNK // SUB
ROW_FACTORS = 24


def _hgrn_decay_matrix(reverse):
    m = np.zeros((2 * CHUNK + 32, CHUNK), np.float32)
    r = 2 * CHUNK
    for i in range(N_SUB):
        lo, hi, mid = i * SUB, (i + 1) * SUB, i * SUB + SUB // 2
        for t in range(lo, hi):
            if reverse:
                m[t, t:hi] = 1
                m[CHUNK + t, lo:t] = 1
            else:
                m[t, lo:t + 1] = 1
                m[CHUNK + t, t + 1:hi] = 1
        if reverse:
            before, after, first, second = slice(hi, CHUNK), slice(0, lo), slice(mid, hi), slice(lo, mid)
        else:
            before, after, first, second = slice(0, lo), slice(hi, CHUNK), slice(lo, mid), slice(mid, hi)
        m[r + i, before] = 1
        m[r + 4 + i, after] = 1
        m[r + 8 + i, first] = -1
        m[r + 12 + i, second] = -1
    b2, b1 = (1, 2) if reverse else (2, 1)
    m[r + 16, b2 * SUB:(b2 + 1) * SUB] = 1
    m[r + 17, b1 * SUB:(b1 + 1) * SUB] = 1
    m[r + 18, :] = 1
    return np.concatenate([m, m], axis=1)


def _cumsum_rows(tri_bf, g):
    g1 = g.astype(BF16)
    r1 = g - g1.astype(F32)
    g2 = r1.astype(BF16)
    g3 = (r1 - g2.astype(F32)).astype(BF16)
    dot = functools.partial(jnp.dot, preferred_element_type=F32)
    return dot(tri_bf, g1) + dot(tri_bf, g2) + dot(tri_bf, g3)


def _hgrn_kernel(lbl_ref, dm_ref, cz_ref, cv_ref, q_ref, z_ref, v_ref, *rest, reverse, finalize,
                 heads, chunks_per_step):
    if finalize:
        hg_ref, ob_ref, gain_ref, o_ref, st_ref = rest
    else:
        o_ref, st_ref = rest
    t = pl.program_id(2)
    c = CHUNK
    row = lax.broadcasted_iota(jnp.int32, (c, c), 0)
    col = lax.broadcasted_iota(jnp.int32, (c, c), 1)
    tri = (col >= row) if reverse else (col <= row)
    end = 0 if reverse else c - 1

    l0, l1 = lbl_ref[0:1, :], lbl_ref[1:2, :]
    lm = jnp.maximum(l0, l1)
    e0, e1 = jnp.exp(l0 - lm), jnp.exp(l1 - lm)
    lb = e0 / (e0 + e1)
    fa, fb = lb + 0.5 * (1.0 - lb), 0.5 * (1.0 - lb)

    def forget(z):
        f = fa + fb * jnp.tanh(0.5 * z.astype(F32))
        return jnp.log(f), 1.0 - f

    @pl.when(t == 0)
    def _():
        tri_bf = jnp.where(tri, 1.0, 0.0).astype(BF16)
        n_cc = cz_ref.shape[0] // c
        s_ctx = [jnp.zeros((HEAD, HEAD), F32) for _ in range(heads)]
        for ci in (range(n_cc - 1, -1, -1) if reverse else range(n_cc)):
            rs = slice(ci * c, (ci + 1) * c)
            g, kk = forget(cz_ref[rs, :])
            gc = _cumsum_rows(tri_bf, g)
            for h in range(heads):
                hs = slice(h * HEAD, (h + 1) * HEAD)
                gend = gc[end:end + 1, hs]
                ks = (kk[:, hs] * jnp.exp(gend - gc[:, hs])).astype(BF16)
                s_ctx[h] = jnp.exp(gend) * s_ctx[h] + _dot_tn(cv_ref[rs, hs], ks)
        for h in range(heads):
            st_ref[h] = s_ctx[h]

    def blk(x, i):
        return x[i * SUB:(i + 1) * SUB]

    def tb(tau):
        return N_SUB - 1 - tau if reverse else tau

    zero = jnp.zeros((SUB, HEAD), BF16)

    def place(parts):
        return jnp.concatenate([parts.get(i, zero) for i in range(N_SUB)], axis=0)

    diag = jnp.logical_and(tri, (row ^ col) < SUB)

    def scores(ci):
        rs = slice(ci * c, (ci + 1) * c)
        g, kk = forget(z_ref[rs, :])
        g1 = g.astype(BF16)
        g2 = (g - g1.astype(F32)).astype(BF16)
        ld = jnp.dot(dm_ref[...], jnp.concatenate([g1, g2], axis=0), preferred_element_type=F32)
        hq = 0.5 * q_ref[rs, :].astype(F32)
        q_hat = (hq + hq * jnp.tanh(hq)) * jnp.exp(ld[0:c])
        k_hat = kk * jnp.exp(ld[c:2 * c])
        rf = jnp.exp(ld[2 * c:2 * c + ROW_FACTORS])
        per_head = []
        for h in range(heads):
            hs = slice(h * HEAD, (h + 1) * HEAD)
            q_h, k_h = q_hat[:, hs], k_hat[:, hs]

            def scaled(x, r0):
                return jnp.concatenate(
                    [blk(x, i) * rf[r0 + i:r0 + i + 1, hs] for i in range(N_SUB)],
                    axis=0).astype(BF16)

            qi, ks, qd, kd = scaled(q_h, 0), scaled(k_h, 4), scaled(q_h, 8), scaled(k_h, 12)
            qb, kb = q_h.astype(BF16), k_h.astype(BF16)
            q3 = (blk(q_h, tb(3)) * rf[16:17, hs]).astype(BF16)
            k0 = (blk(k_h, tb(0)) * rf[17:18, hs]).astype(BF16)
            ql = jnp.concatenate(
                [place({tb(1): blk(qb, tb(1))}), place({tb(3): blk(qb, tb(3))}),
                 place({tb(2): blk(qb, tb(2)), tb(3): q3})], axis=1)
            kl = jnp.concatenate(
                [place({tb(0): blk(kb, tb(0))}), place({tb(2): blk(kb, tb(2))}),
                 place({tb(1): blk(kb, tb(1)), tb(0): k0})], axis=1)
            per_head.append((qi, ks, _dot_nt(ql, kl), _dot_nt(qd, kd), rf[18:19, hs]))
        return per_head

    def outputs(ci, per_head, state):
        rs = slice(ci * c, (ci + 1) * c)
        a_bf = [jnp.where(diag, a_dg, a_off).astype(BF16) for _, _, a_off, a_dg, _ in per_head]
        outs, new_state = [], []
        for h in range(heads):
            hs = slice(h * HEAD, (h + 1) * HEAD)
            qi, ks, _, _, dec = per_head[h]
            v_h = v_ref[rs, hs]
            outs.append(_dot_nt(qi, state[h].astype(BF16))
                        + jnp.dot(a_bf[h], v_h, preferred_element_type=F32))
            new_state.append(dec * state[h] + _dot_tn(v_h, ks))
        for h in range(heads):
            hs = slice(h * HEAD, (h + 1) * HEAD)
            if finalize:
                ot = outs[h] + ob_ref[rs, hs]
                y = ot * lax.rsqrt(jnp.mean(ot * ot, axis=-1, keepdims=True) + EPS)
                y = y * gain_ref[:, hs]
                o_ref[rs, hs] = (y * _silu(hg_ref[rs, hs].astype(F32))).astype(o_ref.dtype)
            else:
                o_ref[rs, hs] = outs[h]
        return new_state

    state = [st_ref[h] for h in range(heads)]
    order = list(range(chunks_per_step - 1, -1, -1) if reverse else range(chunks_per_step))
    pending = scores(order[0])
    for n, ci in enumerate(order):
        upcoming = scores(order[n + 1]) if n + 1 < len(order) else None
        state = outputs(ci, pending, state)
        pending = upcoming
    for h in range(heads):
        st_ref[h] = state[h]


def _hgrn_pass(p_lat, p_ctx, lb_logits, o_back, gain, batch, seq, n_ctx, *, reverse):
    finalize = o_back is not None
    heads = 8
    hw = heads * HEAD
    cps = 16
    tb = cps * CHUNK
    steps = seq // tb
    hblocks = HG_HEADS // heads
    dm = jnp.asarray(_hgrn_decay_matrix(reverse), BF16)

    def rblk(b, t):
        return b * steps + ((steps - 1 - t) if reverse else t)

    def lat(col0):
        return pl.BlockSpec((tb, hw), lambda b, h, t: (rblk(b, t), col0 // hw + h))

    z_lat, z_ctx = (P_HFB, C_HFB) if reverse else (P_HFF, C_HFF)
    in_specs = [pl.BlockSpec((2, hw), lambda b, h, t: (0, h)),
                pl.BlockSpec(dm.shape, lambda b, h, t: (0, 0)),
                pl.BlockSpec((n_ctx, hw), lambda b, h, t: (b, z_ctx // hw + h)),
                pl.BlockSpec((n_ctx, hw), lambda b, h, t: (b, C_HI // hw + h)),
                lat(P_HQ), lat(z_lat), lat(P_HI)]
    args = [lb_logits, dm, p_ctx, p_ctx, p_lat, p_lat, p_lat]
    if finalize:
        in_specs += [lat(P_HG), lat(0), pl.BlockSpec((1, hw), lambda b, h, t: (0, h))]
        args += [p_lat, o_back, gain]
    return pl.pallas_call(
        functools.partial(_hgrn_kernel, reverse=reverse, finalize=finalize, heads=heads,
                          chunks_per_step=cps),
        grid=(batch, hblocks, steps),
        in_specs=in_specs,
        out_specs=lat(0),
        out_shape=jax.ShapeDtypeStruct((batch * seq, HG_HEADS * HEAD), BF16 if finalize else F32),
        scratch_shapes=[pltpu.VMEM((heads, HEAD, HEAD), F32)],
        compiler_params=pltpu.CompilerParams(
            dimension_semantics=("arbitrary", "arbitrary", "arbitrary"),
            vmem_limit_bytes=VMEM_LIMIT),
        name="hgrn_fwd" if finalize else "hgrn_bwd",
    )(*args)


def _merge_kernel(hg_ref, at_ref, wh_ref, wa_ref, ma_ref, mb_ref, bm_ref, o_ref):
    half = hg_ref.shape[0] // 2
    halves = [slice(0, half), slice(half, 2 * half)]
    prods = [(jnp.dot(hg_ref[r, :], wh_ref[...], preferred_element_type=F32),
              jnp.dot(at_ref[r, :], wa_ref[...], preferred_element_type=F32)) for r in halves]
    for r, (ya, yb) in zip(halves, prods):
        ga = jax.nn.sigmoid(ma_ref[r, :].astype(F32) + bm_ref[0:1, :])
        gb = jax.nn.sigmoid(mb_ref[r, :].astype(F32) + bm_ref[1:2, :])
        o_ref[r, :] = (ga * ya + gb * yb).astype(o_ref.dtype)


def _merge(hg_branch, att_branch, wh_bf, wa_bf, p_lat, b_merge):
    m = hg_branch.shape[0]
    tm, tn = 1024, 512
    return pl.pallas_call(
        _merge_kernel,
        grid=(m // tm, D_MODEL // tn),
        in_specs=[pl.BlockSpec((tm, D_MODEL), lambda i, j: (i, 0)),
                  pl.BlockSpec((tm, D_MODEL), lambda i, j: (i, 0)),
                  pl.BlockSpec((D_MODEL, tn), lambda i, j: (0, j)),
                  pl.BlockSpec((D_MODEL, tn), lambda i, j: (0, j)),
                  pl.BlockSpec((tm, tn), lambda i, j: (i, P_MA // tn + j)),
                  pl.BlockSpec((tm, tn), lambda i, j: (i, P_MB // tn + j)),
                  pl.BlockSpec((2, tn), lambda i, j: (0, j))],
        out_specs=pl.BlockSpec((tm, tn), lambda i, j: (i, j)),
        out_shape=jax.ShapeDtypeStruct((m, D_MODEL), BF16),
        compiler_params=pltpu.CompilerParams(
            dimension_semantics=("arbitrary", "arbitrary"), vmem_limit_bytes=VMEM_LIMIT),
        name="merge",
    )(hg_branch, att_branch, wh_bf, wa_bf, p_lat, p_lat, b_merge)


def _out_kernel(y_ref, w_ref, x_ref, mod_ref, gain_ref, o_ref):
    half = y_ref.shape[0] // 2
    halves = [slice(0, half), slice(half, 2 * half)]
    zs = [jnp.dot(y_ref[r, :], w_ref[...], preferred_element_type=F32) for r in halves]
    gate = mod_ref[0, :, 2 * D_MODEL:3 * D_MODEL]
    for r, z in zip(halves, zs):
        h = x_ref[r, :] + gate * z
        y = h * lax.rsqrt(jnp.mean(h * h, axis=-1, keepdims=True) + EPS)
        o_ref[r, :] = y * gain_ref[...]


def _out(y, wo_bf, x2, mod3, fgain, seq):
    m = y.shape[0]
    tm = 512
    tiles_per_seq = seq // tm
    return pl.pallas_call(
        _out_kernel,
        grid=(m // tm,),
        in_specs=[pl.BlockSpec((tm, D_MODEL), lambda i: (i, 0)),
                  pl.BlockSpec((D_MODEL, D_MODEL), lambda i: (0, 0)),
                  pl.BlockSpec((tm, D_MODEL), lambda i: (i, 0)),
                  pl.BlockSpec((1, 1, 3 * D_MODEL), lambda i: (i // tiles_per_seq, 0, 0)),
                  pl.BlockSpec((1, D_MODEL), lambda i: (0, 0))],
        out_specs=pl.BlockSpec((tm, D_MODEL), lambda i: (i, 0)),
        out_shape=jax.ShapeDtypeStruct((m, D_MODEL), F32),
        compiler_params=pltpu.CompilerParams(
            dimension_semantics=("arbitrary",), vmem_limit_bytes=VMEM_LIMIT),
        name="out",
    )(y, wo_bf, x2, mod3, fgain)


def _rope_tables(seq):
    rows = seq // GRID_W
    row = jnp.repeat(jnp.arange(rows, dtype=F32), GRID_W)
    col = jnp.tile(jnp.arange(GRID_W, dtype=F32), rows)
    half = HEAD // 2
    inv_freq = 1.0 / (ROPE_BASE ** (jnp.arange(0, half, 2, dtype=F32) / half))
    ang_r = row[:, None] * inv_freq[None, :]
    ang_c = col[:, None] * inv_freq[None, :]
    ang = jnp.concatenate([ang_r, ang_c, ang_r, ang_c], axis=-1)
    sign = jnp.where(jnp.arange(HEAD) < half, -1.0, 1.0)
    return jnp.cos(ang), jnp.sin(ang) * sign


def kernel(x, c, ctx, c_ctx, w_ada, b_ada, norm_gain, w_in, b_merge, lb_logits_fwd,
           lb_logits_bwd, hgrn_norm_gain, w_o_hgrn, sink_logits, w_o_attn, w_out,
           final_norm_gain):
    batch, seq, d = x.shape
    n_ctx = ctx.shape[1]
    assert d == D_MODEL and w_ada.shape[0] == 1 and seq % 1024 == 0 and batch * n_ctx == 1024

    c8 = jnp.concatenate([c, c_ctx[None, :], jnp.zeros((8 - batch - 1, d), F32)], axis=0)
    mod8 = _ada(c8, w_ada[0], b_ada[0][None, :])
    mod3 = mod8[:batch, None, :]
    modc3 = mod8[batch:batch + 1, None, :]

    cos, sin = _rope_tables(seq)
    gain = norm_gain[0][None, :]
    x2 = x.reshape(batch * seq, d)
    w_bf, p_ctx = _wcast_ctx(ctx.reshape(batch * n_ctx, d), modc3, gain, w_in[0])
    p_lat = _inproj_latent(x2, mod3, gain, cos, sin, w_bf, seq)

    att_branch = _attention(p_lat, p_ctx, sink_logits[0][None, :], batch, seq, n_ctx)

    o_back = _hgrn_pass(p_lat, p_ctx, lb_logits_bwd, None, None, batch, seq, n_ctx, reverse=True)
    hg_branch = _hgrn_pass(p_lat, p_ctx, lb_logits_fwd, o_back, hgrn_norm_gain[0][None, :],
                           batch, seq, n_ctx, reverse=False)

    y = _merge(hg_branch, att_branch, w_o_hgrn[0].astype(BF16), w_o_attn[0].astype(BF16),
               p_lat, b_merge[0])
    out = _out(y, w_out[0].astype(BF16), x2, mod3, final_norm_gain[None, :], seq)
    return out.reshape(batch, seq, d)
```

```python
import functools

import jax
import jax.numpy as jnp
import numpy as np
from jax import lax
from jax.experimental import pallas as pl
from jax.experimental.pallas import tpu as pltpu

F32 = jnp.float32
BF16 = jnp.bfloat16

D_MODEL = 2048
GRID_W = 64
EPS = 1e-6
HEAD = 128
ATT_HEADS = 16
ATT_KV_HEADS = 4
ATT_GROUP = ATT_HEADS // ATT_KV_HEADS
WINDOW = 128
ROPE_BASE = 10000.0
LOG2_E = 1.4426950408889634
HG_HEADS = 16
CHUNK = 64
IN_COLS = 19456

P_Q, P_AG, P_HQ, P_HFF, P_HFB, P_HI, P_HG, P_MA, P_MB, P_K, P_V = (
    0, 2048, 4096, 6144, 8192, 10240, 12288, 14336, 16384, 18432, 18944)
C_K, C_V, C_HFF, C_HFB, C_HI, C_COLS = 0, 512, 1024, 3072, 5120, 7168

VMEM_LIMIT = 56 * 1024 * 1024


def _silu(x):
    h = 0.5 * x
    return h + h * jnp.tanh(h)


def _dot_nt(a, b):
    return lax.dot_general(a, b, (((1,), (1,)), ((), ())), preferred_element_type=F32)


def _dot_tn(a, b):
    return lax.dot_general(a, b, (((0,), (0,)), ((), ())), preferred_element_type=F32)


def _ada_kernel(c_ref, w_ref, b_ref, o_ref):
    s = _silu(c_ref[...])
    o_ref[...] = jnp.dot(s.astype(BF16), w_ref[...].astype(BF16),
                         preferred_element_type=F32) + b_ref[...]


def _ada(c8, w_ada, b_ada):
    n_out = w_ada.shape[1]
    tn = 1024
    return pl.pallas_call(
        _ada_kernel,
        grid=(n_out // tn,),
        in_specs=[pl.BlockSpec((8, D_MODEL), lambda j: (0, 0)),
                  pl.BlockSpec((D_MODEL, tn), lambda j: (0, j)),
                  pl.BlockSpec((1, tn), lambda j: (0, j))],
        out_specs=pl.BlockSpec((8, tn), lambda j: (0, j)),
        out_shape=jax.ShapeDtypeStruct((8, n_out), F32),
        compiler_params=pltpu.CompilerParams(
            dimension_semantics=("arbitrary",), vmem_limit_bytes=VMEM_LIMIT),
        name="ada",
    )(c8, w_ada, b_ada)


QUARTER = HEAD // 4


def _pair_lanes(w):
    lane = lax.broadcasted_iota(jnp.int32, (w.shape[0], HEAD), 1)
    second = jnp.logical_and(lane >= QUARTER, lane < 2 * QUARTER)
    third = jnp.logical_and(lane >= 2 * QUARTER, lane < 3 * QUARTER)
    heads = []
    for h in range(w.shape[1] // HEAD):
        t = w[:, h * HEAD:(h + 1) * HEAD]
        up = pltpu.roll(t, HEAD - QUARTER, 1)
        down = pltpu.roll(t, QUARTER, 1)
        heads.append(jnp.where(second, up, jnp.where(third, down, t)))
    return jnp.concatenate(heads, axis=1)


def _rope(t, cos, sin):
    return t * cos + pltpu.roll(t, HEAD // 2, 1) * sin


def _norm_modulate(x_ref, mod_ref, gain_ref):
    x = x_ref[...]
    y = x * lax.rsqrt(jnp.mean(x * x, axis=-1, keepdims=True) + EPS) * gain_ref[...]
    shift = mod_ref[0, :, 0:D_MODEL]
    scale = mod_ref[0, :, D_MODEL:2 * D_MODEL]
    return (y * (1.0 + scale) + shift).astype(BF16)


def _inproj_kernel(x_ref, mod_ref, gain_ref, cos_ref, sin_ref, w_ref, o_ref, xn_ref,
                   *, tn):
    j = pl.program_id(1)

    @pl.when(j == 0)
    def _():
        xn_ref[...] = _norm_modulate(x_ref, mod_ref, gain_ref)

    half = xn_ref.shape[0] // 2
    halves = [slice(0, half), slice(half, 2 * half)]
    accs = [jnp.dot(xn_ref[r, :], w_ref[...], preferred_element_type=F32) for r in halves]
    for r, acc in zip(halves, accs):
        o_ref[r, :] = acc.astype(o_ref.dtype)

    n_heads = tn // HEAD
    q_tiles = P_AG // tn
    kv_tile = P_K // tn
    k_heads = (P_V - P_K) // HEAD
    q_scale = HEAD ** -0.5 * LOG2_E

    def store_rope(n_rot, mult):
        for r, acc in zip(halves, accs):
            for h in range(n_rot):
                sl = slice(h * HEAD, (h + 1) * HEAD)
                rot = _rope(acc[:, sl], cos_ref[r, :], sin_ref[r, :])
                if mult is not None:
                    rot = rot * mult
                o_ref[r, sl] = rot.astype(o_ref.dtype)

    @pl.when(j < q_tiles)
    def _():
        store_rope(n_heads, q_scale)

    @pl.when(j == kv_tile)
    def _():
        store_rope(k_heads, None)


def _inproj_latent(x2, mod3, gain, cos, sin, w_bf, seq):
    m = x2.shape[0]
    tm, tn = 1024, 1024
    tiles_per_seq = seq // tm
    n_tiles = IN_COLS // tn

    def w_map(i, j):
        src = jnp.where(j < 2, j, jnp.where(j == n_tiles - 1, 2, j + 1))
        return (0, src)

    return pl.pallas_call(
        functools.partial(_inproj_kernel, tn=tn),
        grid=(m // tm, n_tiles),
        in_specs=[pl.BlockSpec((tm, D_MODEL), lambda i, j: (i, 0)),
                  pl.BlockSpec((1, 1, 3 * D_MODEL), lambda i, j: (i // tiles_per_seq, 0, 0)),
                  pl.BlockSpec((1, D_MODEL), lambda i, j: (0, 0)),
                  pl.BlockSpec((tm, HEAD), lambda i, j: (i % tiles_per_seq, 0)),
                  pl.BlockSpec((tm, HEAD), lambda i, j: (i % tiles_per_seq, 0)),
                  pl.BlockSpec((D_MODEL, tn), w_map)],
        out_specs=pl.BlockSpec((tm, tn), lambda i, j: (i, j)),
        out_shape=jax.ShapeDtypeStruct((m, IN_COLS), BF16),
        scratch_shapes=[pltpu.VMEM((tm, D_MODEL), BF16)],
        compiler_params=pltpu.CompilerParams(
            dimension_semantics=("arbitrary", "arbitrary"), vmem_limit_bytes=VMEM_LIMIT),
        name="inproj",
    )(x2, mod3, gain, cos, sin, w_bf)


CTX_TN = 512
W_K, W_KV_END, W_FF, W_I_END = 2048, 3072, 7168, 13312
CTX_KV_TILES = (W_K // CTX_TN, W_KV_END // CTX_TN)
CTX_HG_TILES = (W_FF // CTX_TN, W_I_END // CTX_TN)
CTX_HG_SLOT0 = C_HFF // CTX_TN


def _ctx_tile_needed(j):
    return jnp.logical_or(jnp.logical_and(j >= CTX_KV_TILES[0], j < CTX_KV_TILES[1]),
                          jnp.logical_and(j >= CTX_HG_TILES[0], j < CTX_HG_TILES[1]))


def _ctx_out_slot(j):
    kv = jnp.clip(j - CTX_KV_TILES[0], 0, CTX_KV_TILES[1] - CTX_KV_TILES[0] - 1)
    hg = jnp.clip(j - CTX_HG_TILES[0], 0, CTX_HG_TILES[1] - CTX_HG_TILES[0] - 1) + CTX_HG_SLOT0
    return jnp.where(j < CTX_HG_TILES[0], kv, hg)


def _wcast_ctx_kernel(x_ref, mod_ref, gain_ref, w_ref, wb_ref, o_ref, xn_ref):
    j = pl.program_id(0)

    @pl.when(j == 0)
    def _():
        xn_ref[...] = _norm_modulate(x_ref, mod_ref, gain_ref)

    def emit(wb):
        wb_ref[...] = wb

        @pl.when(_ctx_tile_needed(j))
        def _():
            o_ref[...] = jnp.dot(xn_ref[...], wb, preferred_element_type=F32).astype(o_ref.dtype)

    rotary = j < CTX_KV_TILES[0] + (P_V - P_K) // CTX_TN

    @pl.when(rotary)
    def _():
        emit(_pair_lanes(w_ref[...]).astype(BF16))

    @pl.when(jnp.logical_not(rotary))
    def _():
        emit(w_ref[...].astype(BF16))


def _wcast_ctx(ctx2, modc3, gain, w_in):
    m = ctx2.shape[0]
    tn = CTX_TN
    return pl.pallas_call(
        _wcast_ctx_kernel,
        grid=(IN_COLS // tn,),
        in_specs=[pl.BlockSpec((m, D_MODEL), lambda j: (0, 0)),
                  pl.BlockSpec((1, 1, 3 * D_MODEL), lambda j: (0, 0, 0)),
                  pl.BlockSpec((1, D_MODEL), lambda j: (0, 0)),
                  pl.BlockSpec((D_MODEL, tn), lambda j: (0, j))],
        out_specs=[pl.BlockSpec((D_MODEL, tn), lambda j: (0, j)),
                   pl.BlockSpec((m, tn), lambda j: (0, _ctx_out_slot(j)))],
        out_shape=[jax.ShapeDtypeStruct((D_MODEL, IN_COLS), BF16),
                   jax.ShapeDtypeStruct((m, C_COLS), BF16)],
        scratch_shapes=[pltpu.VMEM((m, D_MODEL), BF16)],
        compiler_params=pltpu.CompilerParams(
            dimension_semantics=("arbitrary",), vmem_limit_bytes=VMEM_LIMIT),
        name="wcast_ctx",
    )(ctx2, modc3, gain, w_in)


ATT_QBLOCKS = 4


def _attn_kernel(sink_ref, q_ref, ag_ref, *rest, n_steps):
    n_kv = ATT_QBLOCKS + 2
    k_refs, v_refs = rest[:n_kv], rest[n_kv:2 * n_kv]
    kx_ref, vx_ref, o_ref = rest[2 * n_kv:]
    i = pl.program_id(1)
    qb = WINDOW
    row = lax.broadcasted_iota(jnp.int32, (ATT_GROUP * qb, qb), 0) & (qb - 1)
    col = lax.broadcasted_iota(jnp.int32, (ATT_GROUP * qb, qb), 1)
    ok_prev = [col >= row + (jnp.where(i > 0, 0, 2 * qb) if u == 0 else 0)
               for u in range(ATT_QBLOCKS)]
    ok_next = [col + (jnp.where(i < n_steps - 1, 0, 2 * qb) if u == ATT_QBLOCKS - 1 else 0) <= row
               for u in range(ATT_QBLOCKS)]

    def scores(u, kh):
        ks = slice(kh * HEAD, (kh + 1) * HEAD)
        rs = slice(u * qb, (u + 1) * qb)
        kcat = jnp.concatenate([k_refs[u][:, ks], k_refs[u + 1][:, ks], k_refs[u + 2][:, ks],
                                kx_ref[:, ks]], axis=0)
        q4 = jnp.concatenate(
            [q_ref[rs, (kh * ATT_GROUP + g) * HEAD:(kh * ATT_GROUP + g + 1) * HEAD]
             for g in range(ATT_GROUP)], axis=0)
        return _dot_nt(q4, kcat)

    def finish(u, kh, s):
        ks = slice(kh * HEAD, (kh + 1) * HEAD)
        rs = slice(u * qb, (u + 1) * qb)
        vcat = jnp.concatenate([v_refs[u][:, ks], v_refs[u + 1][:, ks], v_refs[u + 2][:, ks],
                                vx_ref[:, ks]], axis=0)
        heads = [kh * ATT_GROUP + g for g in range(ATT_GROUP)]
        s = jnp.concatenate(
            [jnp.where(ok_prev[u], s[:, 0:qb], -jnp.inf), s[:, qb:2 * qb],
             jnp.where(ok_next[u], s[:, 2 * qb:3 * qb], -jnp.inf), s[:, 3 * qb:]], axis=1)
        sink = jnp.concatenate(
            [jnp.full((qb, 1), sink_ref[0, h] * LOG2_E, F32) for h in heads], axis=0)
        m = jnp.maximum(jnp.max(s, axis=-1, keepdims=True), sink)
        p = jnp.exp2(s - m)
        den = jnp.sum(p, axis=-1, keepdims=True) + jnp.exp2(sink - m)
        o4 = jnp.dot(p.astype(BF16), vcat, preferred_element_type=F32) / den
        for g, h in enumerate(heads):
            hs = slice(h * HEAD, (h + 1) * HEAD)
            gate = _silu(ag_ref[rs, hs].astype(F32))
            o_ref[rs, hs] = (o4[g * qb:(g + 1) * qb, :] * gate).astype(o_ref.dtype)

    work = [(u, kh) for u in range(ATT_QBLOCKS) for kh in range(ATT_KV_HEADS)]
    pending = scores(*work[0])
    for n, (u, kh) in enumerate(work):
        upcoming = scores(*work[n + 1]) if n + 1 < len(work) else None
        finish(u, kh, pending)
        pending = upcoming


def _attention(p_lat, p_ctx, sink, batch, seq, n_ctx):
    qb = WINDOW
    nb = seq // qb
    steps = nb // ATT_QBLOCKS
    kvw = ATT_KV_HEADS * HEAD
    kcol, vcol = P_K // kvw, P_V // kvw

    def kv_spec(off, c):
        return pl.BlockSpec(
            (qb, kvw), lambda b, i: (b * nb + jnp.clip(i * ATT_QBLOCKS + off, 0, nb - 1), c))

    def q_spec(c):
        return pl.BlockSpec((ATT_QBLOCKS * qb, D_MODEL), lambda b, i: (b * steps + i, c))

    offs = range(-1, ATT_QBLOCKS + 1)
    return pl.pallas_call(
        functools.partial(_attn_kernel, n_steps=steps),
        grid=(batch, steps),
        in_specs=[pl.BlockSpec(memory_space=pltpu.SMEM),
                  q_spec(P_Q // D_MODEL), q_spec(P_AG // D_MODEL)]
        + [kv_spec(off, kcol) for off in offs] + [kv_spec(off, vcol) for off in offs]
        + [pl.BlockSpec((n_ctx, kvw), lambda b, i: (b, C_K // kvw)),
           pl.BlockSpec((n_ctx, kvw), lambda b, i: (b, C_V // kvw))],
        out_specs=q_spec(0),
        out_shape=jax.ShapeDtypeStruct((batch * seq, D_MODEL), BF16),
        compiler_params=pltpu.CompilerParams(
            dimension_semantics=("arbitrary", "arbitrary"), vmem_limit_bytes=VMEM_LIMIT),
        name="attn",
    )(sink, p_lat, p_lat, *([p_lat] * (2 * (ATT_QBLOCKS + 2))), p_ctx, p_ctx)


SUB = 16
N_SUB = CHUNK // SUB
ROW_FACTORS = 24


def _hgrn_decay_matrix(reverse):
    m = np.zeros((2 * CHUNK + 32, CHUNK), np.float32)
    r = 2 * CHUNK
    for i in range(N_SUB):
        lo, hi, mid = i * SUB, (i + 1) * SUB, i * SUB + SUB // 2
        for t in range(lo, hi):
            if reverse:
                m[t, t:hi] = 1
                m[CHUNK + t, lo:t] = 1
            else:
                m[t, lo:t + 1] = 1
                m[CHUNK + t, t + 1:hi] = 1
        if reverse:
            before, after, first, second = slice(hi, CHUNK), slice(0, lo), slice(mid, hi), slice(lo, mid)
        else:
            before, after, first, second = slice(0, lo), slice(hi, CHUNK), slice(lo, mid), slice(mid, hi)
        m[r + i, before] = 1
        m[r + 4 + i, after] = 1
        m[r + 8 + i, first] = -1
        m[r + 12 + i, second] = -1
    b2, b1 = (1, 2) if reverse else (2, 1)
    m[r + 16, b2 * SUB:(b2 + 1) * SUB] = 1
    m[r + 17, b1 * SUB:(b1 + 1) * SUB] = 1
    m[r + 18, :] = 1
    return np.concatenate([m, m], axis=1)


def _cumsum_rows(tri_bf, g):
    g1 = g.astype(BF16)
    r1 = g - g1.astype(F32)
    g2 = r1.astype(BF16)
    g3 = (r1 - g2.astype(F32)).astype(BF16)
    dot = functools.partial(jnp.dot, preferred_element_type=F32)
    return dot(tri_bf, g1) + dot(tri_bf, g2) + dot(tri_bf, g3)


def _hgrn_kernel(lbl_ref, dm_ref, cz_ref, cv_ref, q_ref, z_ref, v_ref, *rest, reverse, finalize,
                 heads, chunks_per_step):
    if finalize:
        hg_ref, ob_ref, gain_ref, o_ref, st_ref = rest
    else:
        o_ref, st_ref = rest
    t = pl.program_id(2)
    c = CHUNK
    row = lax.broadcasted_iota(jnp.int32, (c, c), 0)
    col = lax.broadcasted_iota(jnp.int32, (c, c), 1)
    tri = (col >= row) if reverse else (col <= row)
    end = 0 if reverse else c - 1

    l0, l1 = lbl_ref[0:1, :], lbl_ref[1:2, :]
    lm = jnp.maximum(l0, l1)
    e0, e1 = jnp.exp(l0 - lm), jnp.exp(l1 - lm)
    lb = e0 / (e0 + e1)
    fa, fb = lb + 0.5 * (1.0 - lb), 0.5 * (1.0 - lb)

    def forget(z):
        f = fa + fb * jnp.tanh(0.5 * z.astype(F32))
        return jnp.log(f), 1.0 - f

    @pl.when(t == 0)
    def _():
        tri_bf = jnp.where(tri, 1.0, 0.0).astype(BF16)
        n_cc = cz_ref.shape[0] // c
        s_ctx = [jnp.zeros((HEAD, HEAD), F32) for _ in range(heads)]
        for ci in (range(n_cc - 1, -1, -1) if reverse else range(n_cc)):
            rs = slice(ci * c, (ci + 1) * c)
            g, kk = forget(cz_ref[rs, :])
            gc = _cumsum_rows(tri_bf, g)
            for h in range(heads):
                hs = slice(h * HEAD, (h + 1) * HEAD)
                gend = gc[end:end + 1, hs]
                ks = (kk[:, hs] * jnp.exp(gend - gc[:, hs])).astype(BF16)
                s_ctx[h] = jnp.exp(gend) * s_ctx[h] + _dot_tn(cv_ref[rs, hs], ks)
        for h in range(heads):
            st_ref[h] = s_ctx[h]

    def blk(x, i):
        return x[i * SUB:(i + 1) * SUB]

    def tb(tau):
        return N_SUB - 1 - tau if reverse else tau

    zero = jnp.zeros((SUB, HEAD), BF16)

    def place(parts):
        return jnp.concatenate([parts.get(i, zero) for i in range(N_SUB)], axis=0)

    diag = jnp.logical_and(tri, (row ^ col) < SUB)

    def scores(ci):
        rs = slice(ci * c, (ci + 1) * c)
        g, kk = forget(z_ref[rs, :])
        g1 = g.astype(BF16)
        g2 = (g - g1.astype(F32)).astype(BF16)
        ld = jnp.dot(dm_ref[...], jnp.concatenate([g1, g2], axis=0), preferred_element_type=F32)
        hq = 0.5 * q_ref[rs, :].astype(F32)
        q_hat = (hq + hq * jnp.tanh(hq)) * jnp.exp(ld[0:c])
        k_hat = kk * jnp.exp(ld[c:2 * c])
        rf = jnp.exp(ld[2 * c:2 * c + ROW_FACTORS])
        per_head = []
        for h in range(heads):
            hs = slice(h * HEAD, (h + 1) * HEAD)
            q_h, k_h = q_hat[:, hs], k_hat[:, hs]

            def scaled(x, r0):
                return jnp.concatenate(
                    [blk(x, i) * rf[r0 + i:r0 + i + 1, hs] for i in range(N_SUB)],
                    axis=0).astype(BF16)

            qi, ks, qd, kd = scaled(q_h, 0), scaled(k_h, 4), scaled(q_h, 8), scaled(k_h, 12)
            qb, kb = q_h.astype(BF16), k_h.astype(BF16)
            q3 = (blk(q_h, tb(3)) * rf[16:17, hs]).astype(BF16)
            k0 = (blk(k_h, tb(0)) * rf[17:18, hs]).astype(BF16)
            ql = jnp.concatenate(
                [place({tb(1): blk(qb, tb(1))}), place({tb(3): blk(qb, tb(3))}),
                 place({tb(2): blk(qb, tb(2)), tb(3): q3})], axis=1)
            kl = jnp.concatenate(
                [place({tb(0): blk(kb, tb(0))}), place({tb(2): blk(kb, tb(2))}),
                 place({tb(1): blk(kb, tb(1)), tb(0): k0})], axis=1)
            per_head.append((qi, ks, _dot_nt(ql, kl), _dot_nt(qd, kd), rf[18:19, hs]))
        return per_head

    def outputs(ci, per_head, state):
        rs = slice(ci * c, (ci + 1) * c)
        a_bf = [jnp.where(diag, a_dg, a_off).astype(BF16) for _, _, a_off, a_dg, _ in per_head]
        outs, new_state = [], []
        for h in range(heads):
            hs = slice(h * HEAD, (h + 1) * HEAD)
            qi, ks, _, _, dec = per_head[h]
            v_h = v_ref[rs, hs]
            outs.append(_dot_nt(qi, state[h].astype(BF16))
                        + jnp.dot(a_bf[h], v_h, preferred_element_type=F32))
            new_state.append(dec * state[h] + _dot_tn(v_h, ks))
        for h in range(heads):
            hs = slice(h * HEAD, (h + 1) * HEAD)
            if finalize:
                ot = outs[h] + ob_ref[rs, hs]
                y = ot * lax.rsqrt(jnp.mean(ot * ot, axis=-1, keepdims=True) + EPS)
                y = y * gain_ref[:, hs]
                o_ref[rs, hs] = (y * _silu(hg_ref[rs, hs].astype(F32))).astype(o_ref.dtype)
            else:
                o_ref[rs, hs] = outs[h]
        return new_state

    state = [st_ref[h] for h in range(heads)]
    order = list(range(chunks_per_step - 1, -1, -1) if reverse else range(chunks_per_step))
    pending = scores(order[0])
    for n, ci in enumerate(order):
        upcoming = scores(order[n + 1]) if n + 1 < len(order) else None
        state = outputs(ci, pending, state)
        pending = upcoming
    for h in range(heads):
        st_ref[h] = state[h]


def _hgrn_pass(p_lat, p_ctx, lb_logits, o_back, gain, batch, seq, n_ctx, *, reverse):
    finalize = o_back is not None
    heads = 8
    hw = heads * HEAD
    cps = 16
    tb = cps * CHUNK
    steps = seq // tb
    hblocks = HG_HEADS // heads
    dm = jnp.asarray(_hgrn_decay_matrix(reverse), BF16)

    def rblk(b, t):
        return b * steps + ((steps - 1 - t) if reverse else t)

    def lat(col0):
        return pl.BlockSpec((tb, hw), lambda b, h, t: (rblk(b, t), col0 // hw + h))

    z_lat, z_ctx = (P_HFB, C_HFB) if reverse else (P_HFF, C_HFF)
    in_specs = [pl.BlockSpec((2, hw), lambda b, h, t: (0, h)),
                pl.BlockSpec(dm.shape, lambda b, h, t: (0, 0)),
                pl.BlockSpec((n_ctx, hw), lambda b, h, t: (b, z_ctx // hw + h)),
                pl.BlockSpec((n_ctx, hw), lambda b, h, t: (b, C_HI // hw + h)),
                lat(P_HQ), lat(z_lat), lat(P_HI)]
    args = [lb_logits, dm, p_ctx, p_ctx, p_lat, p_lat, p_lat]
    if finalize:
        in_specs += [lat(P_HG), lat(0), pl.BlockSpec((1, hw), lambda b, h, t: (0, h))]
        args += [p_lat, o_back, gain]
    return pl.pallas_call(
        functools.partial(_hgrn_kernel, reverse=reverse, finalize=finalize, heads=heads,
                          chunks_per_step=cps),
        grid=(batch, hblocks, steps),
        in_specs=in_specs,
        out_specs=lat(0),
        out_shape=jax.ShapeDtypeStruct((batch * seq, HG_HEADS * HEAD), BF16 if finalize else F32),
        scratch_shapes=[pltpu.VMEM((heads, HEAD, HEAD), F32)],
        compiler_params=pltpu.CompilerParams(
            dimension_semantics=("arbitrary", "arbitrary", "arbitrary"),
            vmem_limit_bytes=VMEM_LIMIT),
        name="hgrn_fwd" if finalize else "hgrn_bwd",
    )(*args)


def _merge_kernel(hg_ref, at_ref, wh_ref, wa_ref, ma_ref, mb_ref, bm_ref, o_ref):
    half = hg_ref.shape[0] // 2
    halves = [slice(0, half), slice(half, 2 * half)]
    prods = [(jnp.dot(hg_ref[r, :], wh_ref[...], preferred_element_type=F32),
              jnp.dot(at_ref[r, :], wa_ref[...], preferred_element_type=F32)) for r in halves]
    for r, (ya, yb) in zip(halves, prods):
        ga = jax.nn.sigmoid(ma_ref[r, :].astype(F32) + bm_ref[0:1, :])
        gb = jax.nn.sigmoid(mb_ref[r, :].astype(F32) + bm_ref[1:2, :])
        o_ref[r, :] = (ga * ya + gb * yb).astype(o_ref.dtype)


def _merge(hg_branch, att_branch, wh_bf, wa_bf, p_lat, b_merge):
    m = hg_branch.shape[0]
    tm, tn = 1024, 512
    return pl.pallas_call(
        _merge_kernel,
        grid=(m // tm, D_MODEL // tn),
        in_specs=[pl.BlockSpec((tm, D_MODEL), lambda i, j: (i, 0)),
                  pl.BlockSpec((tm, D_MODEL), lambda i, j: (i, 0)),
                  pl.BlockSpec((D_MODEL, tn), lambda i, j: (0, j)),
                  pl.BlockSpec((D_MODEL, tn), lambda i, j: (0, j)),
                  pl.BlockSpec((tm, tn), lambda i, j: (i, P_MA // tn + j)),
                  pl.BlockSpec((tm, tn), lambda i, j: (i, P_MB // tn + j)),
                  pl.BlockSpec((2, tn), lambda i, j: (0, j))],
        out_specs=pl.BlockSpec((tm, tn), lambda i, j: (i, j)),
        out_shape=jax.ShapeDtypeStruct((m, D_MODEL), BF16),
        compiler_params=pltpu.CompilerParams(
            dimension_semantics=("arbitrary", "arbitrary"), vmem_limit_bytes=VMEM_LIMIT),
        name="merge",
    )(hg_branch, att_branch, wh_bf, wa_bf, p_lat, p_lat, b_merge)


def _out_kernel(y_ref, w_ref, x_ref, mod_ref, gain_ref, o_ref):
    half = y_ref.shape[0] // 2
    halves = [slice(0, half), slice(half, 2 * half)]
    zs = [jnp.dot(y_ref[r, :], w_ref[...], preferred_element_type=F32) for r in halves]
    gate = mod_ref[0, :, 2 * D_MODEL:3 * D_MODEL]
    for r, z in zip(halves, zs):
        h = x_ref[r, :] + gate * z
        y = h * lax.rsqrt(jnp.mean(h * h, axis=-1, keepdims=True) + EPS)
        o_ref[r, :] = y * gain_ref[...]


def _out(y, wo_bf, x2, mod3, fgain, seq):
    m = y.shape[0]
    tm = 512
    tiles_per_seq = seq // tm
    return pl.pallas_call(
        _out_kernel,
        grid=(m // tm,),
        in_specs=[pl.BlockSpec((tm, D_MODEL), lambda i: (i, 0)),
                  pl.BlockSpec((D_MODEL, D_MODEL), lambda i: (0, 0)),
                  pl.BlockSpec((tm, D_MODEL), lambda i: (i, 0)),
                  pl.BlockSpec((1, 1, 3 * D_MODEL), lambda i: (i // tiles_per_seq, 0, 0)),
                  pl.BlockSpec((1, D_MODEL), lambda i: (0, 0))],
        out_specs=pl.BlockSpec((tm, D_MODEL), lambda i: (i, 0)),
        out_shape=jax.ShapeDtypeStruct((m, D_MODEL), F32),
        compiler_params=pltpu.CompilerParams(
            dimension_semantics=("arbitrary",), vmem_limit_bytes=VMEM_LIMIT),
        name="out",
    )(y, wo_bf, x2, mod3, fgain)


def _rope_tables(seq):
    rows = seq // GRID_W
    row = jnp.repeat(jnp.arange(rows, dtype=F32), GRID_W)
    col = jnp.tile(jnp.arange(GRID_W, dtype=F32), rows)
    half = HEAD // 2
    inv_freq = 1.0 / (ROPE_BASE ** (jnp.arange(0, half, 2, dtype=F32) / half))
    ang_r = row[:, None] * inv_freq[None, :]
    ang_c = col[:, None] * inv_freq[None, :]
    ang = jnp.concatenate([ang_r, ang_c, ang_r, ang_c], axis=-1)
    sign = jnp.where(jnp.arange(HEAD) < half, -1.0, 1.0)
    return jnp.cos(ang), jnp.sin(ang) * sign


def kernel(x, c, ctx, c_ctx, w_ada, b_ada, norm_gain, w_in, b_merge, lb_logits_fwd,
           lb_logits_bwd, hgrn_norm_gain, w_o_hgrn, sink_logits, w_o_attn, w_out,
           final_norm_gain):
    batch, seq, d = x.shape
    n_ctx = ctx.shape[1]
    assert d == D_MODEL and w_ada.shape[0] == 1 and seq % 1024 == 0 and batch * n_ctx == 1024

    c8 = jnp.concatenate([c, c_ctx[None, :], jnp.zeros((8 - batch - 1, d), F32)], axis=0)
    mod8 = _ada(c8, w_ada[0], b_ada[0][None, :])
    mod3 = mod8[:batch, None, :]
    modc3 = mod8[batch:batch + 1, None, :]

    cos, sin = _rope_tables(seq)
    gain = norm_gain[0][None, :]
    x2 = x.reshape(batch * seq, d)
    w_bf, p_ctx = _wcast_ctx(ctx.reshape(batch * n_ctx, d), modc3, gain, w_in[0])
    p_lat = _inproj_latent(x2, mod3, gain, cos, sin, w_bf, seq)

    att_branch = _attention(p_lat, p_ctx, sink_logits[0][None, :], batch, seq, n_ctx)

    o_back = _hgrn_pass(p_lat, p_ctx, lb_logits_bwd, None, None, batch, seq, n_ctx, reverse=True)
    hg_branch = _hgrn_pass(p_lat, p_ctx, lb_logits_fwd, o_back, hgrn_norm_gain[0][None, :],
                           batch, seq, n_ctx, reverse=False)

    y = _merge(hg_branch, att_branch, w_o_hgrn[0].astype(BF16), w_o_attn[0].astype(BF16),
               p_lat, b_merge[0])
    out = _out(y, w_out[0].astype(BF16), x2, mod3, final_norm_gain[None, :], seq)
    return out.reshape(batch, seq, d)
```

```python
import functools

import jax
import jax.numpy as jnp
import numpy as np
from jax import lax
from jax.experimental import pallas as pl
from jax.experimental.pallas import tpu as pltpu

F32 = jnp.float32
BF16 = jnp.bfloat16

D_MODEL = 2048
GRID_W = 64
EPS = 1e-6
HEAD = 128
ATT_HEADS = 16
ATT_KV_HEADS = 4
ATT_GROUP = ATT_HEADS // ATT_KV_HEADS
WINDOW = 128
ROPE_BASE = 10000.0
LOG2_E = 1.4426950408889634
HG_HEADS = 16
CHUNK = 64
IN_COLS = 19456

P_Q, P_AG, P_HQ, P_HFF, P_HFB, P_HI, P_HG, P_MA, P_MB, P_K, P_V = (
    0, 2048, 4096, 6144, 8192, 10240, 12288, 14336, 16384, 18432, 18944)
C_HFF, C_HFB, C_HI, C_K, C_V, C_COLS = 0, 2048, 4096, 6144, 6656, 7168

VMEM_LIMIT = 56 * 1024 * 1024


def _silu(x):
    h = 0.5 * x
    return h + h * jnp.tanh(h)


def _dot_nt(a, b):
    return lax.dot_general(a, b, (((1,), (1,)), ((), ())), preferred_element_type=F32)


def _dot_tn(a, b):
    return lax.dot_general(a, b, (((0,), (0,)), ((), ())), preferred_element_type=F32)


def _ada_kernel(c_ref, w_ref, b_ref, o_ref):
    s = _silu(c_ref[...])
    o_ref[...] = jnp.dot(s.astype(BF16), w_ref[...].astype(BF16),
                         preferred_element_type=F32) + b_ref[...]


def _ada(c8, w_ada, b_ada):
    n_out = w_ada.shape[1]
    tn = 1024
    return pl.pallas_call(
        _ada_kernel,
        grid=(n_out // tn,),
        in_specs=[pl.BlockSpec((8, D_MODEL), lambda j: (0, 0)),
                  pl.BlockSpec((D_MODEL, tn), lambda j: (0, j)),
                  pl.BlockSpec((1, tn), lambda j: (0, j))],
        out_specs=pl.BlockSpec((8, tn), lambda j: (0, j)),
        out_shape=jax.ShapeDtypeStruct((8, n_out), F32),
        compiler_params=pltpu.CompilerParams(
            dimension_semantics=("arbitrary",), vmem_limit_bytes=VMEM_LIMIT),
        name="ada",
    )(c8, w_ada, b_ada)


QUARTER = HEAD // 4


def _pair_lanes(w):
    lane = lax.broadcasted_iota(jnp.int32, (w.shape[0], HEAD), 1)
    second = jnp.logical_and(lane >= QUARTER, lane < 2 * QUARTER)
    third = jnp.logical_and(lane >= 2 * QUARTER, lane < 3 * QUARTER)
    heads = []
    for h in range(w.shape[1] // HEAD):
        t = w[:, h * HEAD:(h + 1) * HEAD]
        up = pltpu.roll(t, HEAD - QUARTER, 1)
        down = pltpu.roll(t, QUARTER, 1)
        heads.append(jnp.where(second, up, jnp.where(third, down, t)))
    return jnp.concatenate(heads, axis=1)


def _rope(t, cos, sin):
    return t * cos + pltpu.roll(t, HEAD // 2, 1) * sin


def _norm_modulate(x_ref, mod_ref, gain_ref):
    x = x_ref[...]
    y = x * lax.rsqrt(jnp.mean(x * x, axis=-1, keepdims=True) + EPS) * gain_ref[...]
    shift = mod_ref[0, :, 0:D_MODEL]
    scale = mod_ref[0, :, D_MODEL:2 * D_MODEL]
    return (y * (1.0 + scale) + shift).astype(BF16)


def _inproj_kernel(x_ref, mod_ref, gain_ref, cos_ref, sin_ref, w_ref, o_ref, xn_ref,
                   *, tn):
    j = pl.program_id(1)

    @pl.when(j == 0)
    def _():
        xn_ref[...] = _norm_modulate(x_ref, mod_ref, gain_ref)

    half = xn_ref.shape[0] // 2
    halves = [slice(0, half), slice(half, 2 * half)]
    accs = [jnp.dot(xn_ref[r, :], w_ref[...], preferred_element_type=F32) for r in halves]
    for r, acc in zip(halves, accs):
        o_ref[r, :] = acc.astype(o_ref.dtype)

    n_heads = tn // HEAD
    q_tiles = P_AG // tn
    kv_tile = P_K // tn
    k_heads = (P_V - P_K) // HEAD
    q_scale = HEAD ** -0.5 * LOG2_E

    def store_rope(n_rot, mult):
        for r, acc in zip(halves, accs):
            for h in range(n_rot):
                sl = slice(h * HEAD, (h + 1) * HEAD)
                rot = _rope(acc[:, sl], cos_ref[r, :], sin_ref[r, :])
                if mult is not None:
                    rot = rot * mult
                o_ref[r, sl] = rot.astype(o_ref.dtype)

    @pl.when(j < q_tiles)
    def _():
        store_rope(n_heads, q_scale)

    @pl.when(j == kv_tile)
    def _():
        store_rope(k_heads, None)


def _inproj_latent(x2, mod3, gain, cos, sin, w_bf, seq):
    m = x2.shape[0]
    tm, tn = 1024, 1024
    tiles_per_seq = seq // tm
    n_tiles = IN_COLS // tn

    def w_map(i, j):
        src = jnp.where(j < 2, j, jnp.where(j == n_tiles - 1, 2, j + 1))
        return (0, src)

    return pl.pallas_call(
        functools.partial(_inproj_kernel, tn=tn),
        grid=(m // tm, n_tiles),
        in_specs=[pl.BlockSpec((tm, D_MODEL), lambda i, j: (i, 0)),
                  pl.BlockSpec((1, 1, 3 * D_MODEL), lambda i, j: (i // tiles_per_seq, 0, 0)),
                  pl.BlockSpec((1, D_MODEL), lambda i, j: (0, 0)),
                  pl.BlockSpec((tm, HEAD), lambda i, j: (i % tiles_per_seq, 0)),
                  pl.BlockSpec((tm, HEAD), lambda i, j: (i % tiles_per_seq, 0)),
                  pl.BlockSpec((D_MODEL, tn), w_map)],
        out_specs=pl.BlockSpec((tm, tn), lambda i, j: (i, j)),
        out_shape=jax.ShapeDtypeStruct((m, IN_COLS), BF16),
        scratch_shapes=[pltpu.VMEM((tm, D_MODEL), BF16)],
        compiler_params=pltpu.CompilerParams(
            dimension_semantics=("arbitrary", "arbitrary"), vmem_limit_bytes=VMEM_LIMIT),
        name="inproj",
    )(x2, mod3, gain, cos, sin, w_bf)


CTX_TN = 512
W_K, W_KV_END, W_FF, W_I_END = 2048, 3072, 7168, 13312
CTX_KV_TILES = (W_K // CTX_TN, W_KV_END // CTX_TN)
CTX_HG_TILES = (W_FF // CTX_TN, W_I_END // CTX_TN)
CTX_HG_SLOT0 = C_HFF // CTX_TN


def _ctx_tile_needed(j):
    return jnp.logical_or(jnp.logical_and(j >= CTX_KV_TILES[0], j < CTX_KV_TILES[1]),
                          jnp.logical_and(j >= CTX_HG_TILES[0], j < CTX_HG_TILES[1]))


def _ctx_out_slot(j):
    kv = jnp.clip(j - CTX_KV_TILES[0], 0, CTX_KV_TILES[1] - CTX_KV_TILES[0] - 1) + C_K // CTX_TN
    hg = jnp.clip(j - CTX_HG_TILES[0], 0, CTX_HG_TILES[1] - CTX_HG_TILES[0] - 1) + CTX_HG_SLOT0
    return jnp.where(j < CTX_HG_TILES[0], kv, hg)


def _wcast_ctx_kernel(x_ref, mod_ref, gain_ref, w_ref, wb_ref, o_ref, xn_ref):
    j = pl.program_id(0)

    @pl.when(j == 0)
    def _():
        xn_ref[...] = _norm_modulate(x_ref, mod_ref, gain_ref)

    def emit(wb):
        wb_ref[...] = wb

        @pl.when(_ctx_tile_needed(j))
        def _():
            o_ref[...] = jnp.dot(xn_ref[...], wb, preferred_element_type=F32).astype(o_ref.dtype)

    rotary = j < CTX_KV_TILES[0] + (P_V - P_K) // CTX_TN

    @pl.when(rotary)
    def _():
        emit(_pair_lanes(w_ref[...]).astype(BF16))

    @pl.when(jnp.logical_not(rotary))
    def _():
        emit(w_ref[...].astype(BF16))


def _wcast_ctx(ctx2, modc3, gain, w_in):
    m = ctx2.shape[0]
    tn = CTX_TN
    return pl.pallas_call(
        _wcast_ctx_kernel,
        grid=(IN_COLS // tn,),
        in_specs=[pl.BlockSpec((m, D_MODEL), lambda j: (0, 0)),
                  pl.BlockSpec((1, 1, 3 * D_MODEL), lambda j: (0, 0, 0)),
                  pl.BlockSpec((1, D_MODEL), lambda j: (0, 0)),
                  pl.BlockSpec((D_MODEL, tn), lambda j: (0, j))],
        out_specs=[pl.BlockSpec((D_MODEL, tn), lambda j: (0, j)),
                   pl.BlockSpec((m, tn), lambda j: (0, _ctx_out_slot(j)))],
        out_shape=[jax.ShapeDtypeStruct((D_MODEL, IN_COLS), BF16),
                   jax.ShapeDtypeStruct((m, C_COLS), BF16)],
        scratch_shapes=[pltpu.VMEM((m, D_MODEL), BF16)],
        compiler_params=pltpu.CompilerParams(
            dimension_semantics=("arbitrary",), vmem_limit_bytes=VMEM_LIMIT),
        name="wcast_ctx",
    )(ctx2, modc3, gain, w_in)


ATT_QBLOCKS = 4


def _attn_kernel(sink_ref, q_ref, ag_ref, *rest, n_steps):
    n_kv = ATT_QBLOCKS + 2
    k_refs, v_refs = rest[:n_kv], rest[n_kv:2 * n_kv]
    kx_ref, vx_ref, o_ref = rest[2 * n_kv:]
    i = pl.program_id(1)
    qb = WINDOW
    row = lax.broadcasted_iota(jnp.int32, (ATT_GROUP * qb, qb), 0) & (qb - 1)
    col = lax.broadcasted_iota(jnp.int32, (ATT_GROUP * qb, qb), 1)
    ok_prev = [col >= row + (jnp.where(i > 0, 0, 2 * qb) if u == 0 else 0)
               for u in range(ATT_QBLOCKS)]
    ok_next = [col + (jnp.where(i < n_steps - 1, 0, 2 * qb) if u == ATT_QBLOCKS - 1 else 0) <= row
               for u in range(ATT_QBLOCKS)]

    def scores(u, kh):
        ks = slice(kh * HEAD, (kh + 1) * HEAD)
        rs = slice(u * qb, (u + 1) * qb)
        kcat = jnp.concatenate([k_refs[u][:, ks], k_refs[u + 1][:, ks], k_refs[u + 2][:, ks],
                                kx_ref[:, ks]], axis=0)
        q4 = jnp.concatenate(
            [q_ref[rs, (kh * ATT_GROUP + g) * HEAD:(kh * ATT_GROUP + g + 1) * HEAD]
             for g in range(ATT_GROUP)], axis=0)
        return _dot_nt(q4, kcat)

    def finish(u, kh, s):
        ks = slice(kh * HEAD, (kh + 1) * HEAD)
        rs = slice(u * qb, (u + 1) * qb)
        vcat = jnp.concatenate([v_refs[u][:, ks], v_refs[u + 1][:, ks], v_refs[u + 2][:, ks],
                                vx_ref[:, ks]], axis=0)
        heads = [kh * ATT_GROUP + g for g in range(ATT_GROUP)]
        s = jnp.concatenate(
            [jnp.where(ok_prev[u], s[:, 0:qb], -jnp.inf), s[:, qb:2 * qb],
             jnp.where(ok_next[u], s[:, 2 * qb:3 * qb], -jnp.inf), s[:, 3 * qb:]], axis=1)
        sink = jnp.concatenate(
            [jnp.full((qb, 1), sink_ref[0, h] * LOG2_E, F32) for h in heads], axis=0)
        m = jnp.maximum(jnp.max(s, axis=-1, keepdims=True), sink)
        p = jnp.exp2(s - m)
        den = jnp.sum(p, axis=-1, keepdims=True) + jnp.exp2(sink - m)
        o4 = jnp.dot(p.astype(BF16), vcat, preferred_element_type=F32) / den
        for g, h in enumerate(heads):
            hs = slice(h * HEAD, (h + 1) * HEAD)
            gate = _silu(ag_ref[rs, hs].astype(F32))
            o_ref[rs, hs] = (o4[g * qb:(g + 1) * qb, :] * gate).astype(o_ref.dtype)

    work = [(u, kh) for u in range(ATT_QBLOCKS) for kh in range(ATT_KV_HEADS)]
    pending = scores(*work[0])
    for n, (u, kh) in enumerate(work):
        upcoming = scores(*work[n + 1]) if n + 1 < len(work) else None
        finish(u, kh, pending)
        pending = upcoming


def _attention(p_lat, p_ctx, sink, batch, seq, n_ctx):
    qb = WINDOW
    nb = seq // qb
    steps = nb // ATT_QBLOCKS
    kvw = ATT_KV_HEADS * HEAD
    kcol, vcol = P_K // kvw, P_V // kvw

    def kv_spec(off, c):
        return pl.BlockSpec(
            (qb, kvw), lambda b, i: (b * nb + jnp.clip(i * ATT_QBLOCKS + off, 0, nb - 1), c))

    def q_spec(c):
        return pl.BlockSpec((ATT_QBLOCKS * qb, D_MODEL), lambda b, i: (b * steps + i, c))

    offs = range(-1, ATT_QBLOCKS + 1)
    return pl.pallas_call(
        functools.partial(_attn_kernel, n_steps=steps),
        grid=(batch, steps),
        in_specs=[pl.BlockSpec(memory_space=pltpu.SMEM),
                  q_spec(P_Q // D_MODEL), q_spec(P_AG // D_MODEL)]
        + [kv_spec(off, kcol) for off in offs] + [kv_spec(off, vcol) for off in offs]
        + [pl.BlockSpec((n_ctx, kvw), lambda b, i: (b, C_K // kvw)),
           pl.BlockSpec((n_ctx, kvw), lambda b, i: (b, C_V // kvw))],
        out_specs=q_spec(0),
        out_shape=jax.ShapeDtypeStruct((batch * seq, D_MODEL), BF16),
        compiler_params=pltpu.CompilerParams(
            dimension_semantics=("arbitrary", "arbitrary"), vmem_limit_bytes=VMEM_LIMIT),
        name="attn",
    )(sink, p_lat, p_lat, *([p_lat] * (2 * (ATT_QBLOCKS + 2))), p_ctx, p_ctx)


SUB = 16
N_SUB = CHUNK // SUB
ROW_FACTORS = 24


def _hgrn_decay_matrix(reverse):
    m = np.zeros((2 * CHUNK + 32, CHUNK), np.float32)
    r = 2 * CHUNK
    for i in range(N_SUB):
        lo, hi, mid = i * SUB, (i + 1) * SUB, i * SUB + SUB // 2
        for t in range(lo, hi):
            if reverse:
                m[t, t:hi] = 1
                m[CHUNK + t, lo:t] = 1
            else:
                m[t, lo:t + 1] = 1
                m[CHUNK + t, t + 1:hi] = 1
        if reverse:
            before, after, first, second = slice(hi, CHUNK), slice(0, lo), slice(mid, hi), slice(lo, mid)
        else:
            before, after, first, second = slice(0, lo), slice(hi, CHUNK), slice(lo, mid), slice(mid, hi)
        m[r + i, before] = 1
        m[r + 4 + i, after] = 1
        m[r + 8 + i, first] = -1
        m[r + 12 + i, second] = -1
    b2, b1 = (1, 2) if reverse else (2, 1)
    m[r + 16, b2 * SUB:(b2 + 1) * SUB] = 1
    m[r + 17, b1 * SUB:(b1 + 1) * SUB] = 1
    m[r + 18, :] = 1
    return np.concatenate([m, m], axis=1)


def _cumsum_rows(tri_bf, g):
    g1 = g.astype(BF16)
    r1 = g - g1.astype(F32)
    g2 = r1.astype(BF16)
    g3 = (r1 - g2.astype(F32)).astype(BF16)
    dot = functools.partial(jnp.dot, preferred_element_type=F32)
    return dot(tri_bf, g1) + dot(tri_bf, g2) + dot(tri_bf, g3)


def _hgrn_kernel(lbl_ref, dm_ref, cz_ref, cv_ref, q_ref, z_ref, v_ref, *rest, reverse, finalize,
                 heads, chunks_per_step):
    if finalize:
        hg_ref, ob_ref, gain_ref, o_ref, st_ref = rest
    else:
        o_ref, st_ref = rest
    t = pl.program_id(2)
    c = CHUNK
    row = lax.broadcasted_iota(jnp.int32, (c, c), 0)
    col = lax.broadcasted_iota(jnp.int32, (c, c), 1)
    tri = (col >= row) if reverse else (col <= row)
    end = 0 if reverse else c - 1

    l0, l1 = lbl_ref[0:1, :], lbl_ref[1:2, :]
    lm = jnp.maximum(l0, l1)
    e0, e1 = jnp.exp(l0 - lm), jnp.exp(l1 - lm)
    lb = e0 / (e0 + e1)
    fa, fb = lb + 0.5 * (1.0 - lb), 0.5 * (1.0 - lb)

    def forget(z):
        f = fa + fb * jnp.tanh(0.5 * z.astype(F32))
        return jnp.log(f), 1.0 - f

    @pl.when(t == 0)
    def _():
        tri_bf = jnp.where(tri, 1.0, 0.0).astype(BF16)
        n_cc = cz_ref.shape[0] // c
        s_ctx = [jnp.zeros((HEAD, HEAD), F32) for _ in range(heads)]
        for ci in (range(n_cc - 1, -1, -1) if reverse else range(n_cc)):
            rs = slice(ci * c, (ci + 1) * c)
            g, kk = forget(cz_ref[rs, :])
            gc = _cumsum_rows(tri_bf, g)
            for h in range(heads):
                hs = slice(h * HEAD, (h + 1) * HEAD)
                gend = gc[end:end + 1, hs]
                ks = (kk[:, hs] * jnp.exp(gend - gc[:, hs])).astype(BF16)
                s_ctx[h] = jnp.exp(gend) * s_ctx[h] + _dot_tn(cv_ref[rs, hs], ks)
        for h in range(heads):
            st_ref[h] = s_ctx[h]

    def blk(x, i):
        return x[i * SUB:(i + 1) * SUB]

    def tb(tau):
        return N_SUB - 1 - tau if reverse else tau

    zero = jnp.zeros((SUB, HEAD), BF16)

    def place(parts):
        return jnp.concatenate([parts.get(i, zero) for i in range(N_SUB)], axis=0)

    diag = jnp.logical_and(tri, (row ^ col) < SUB)

    def scores(ci):
        rs = slice(ci * c, (ci + 1) * c)
        g, kk = forget(z_ref[rs, :])
        g1 = g.astype(BF16)
        g2 = (g - g1.astype(F32)).astype(BF16)
        ld = jnp.dot(dm_ref[...], jnp.concatenate([g1, g2], axis=0), preferred_element_type=F32)
        hq = 0.5 * q_ref[rs, :].astype(F32)
        q_hat = (hq + hq * jnp.tanh(hq)) * jnp.exp(ld[0:c])
        k_hat = kk * jnp.exp(ld[c:2 * c])
        rf = jnp.exp(ld[2 * c:2 * c + ROW_FACTORS])
        per_head = []
        for h in range(heads):
            hs = slice(h * HEAD, (h + 1) * HEAD)
            q_h, k_h = q_hat[:, hs], k_hat[:, hs]

            def scaled(x, r0):
                return jnp.concatenate(
                    [blk(x, i) * rf[r0 + i:r0 + i + 1, hs] for i in range(N_SUB)],
                    axis=0).astype(BF16)

            qi, ks, qd, kd = scaled(q_h, 0), scaled(k_h, 4), scaled(q_h, 8), scaled(k_h, 12)
            qb, kb = q_h.astype(BF16), k_h.astype(BF16)
            q3 = (blk(q_h, tb(3)) * rf[16:17, hs]).astype(BF16)
            k0 = (blk(k_h, tb(0)) * rf[17:18, hs]).astype(BF16)
            ql = jnp.concatenate(
                [place({tb(1): blk(qb, tb(1))}), place({tb(3): blk(qb, tb(3))}),
                 place({tb(2): blk(qb, tb(2)), tb(3): q3})], axis=1)
            kl = jnp.concatenate(
                [place({tb(0): blk(kb, tb(0))}), place({tb(2): blk(kb, tb(2))}),
                 place({tb(1): blk(kb, tb(1)), tb(0): k0})], axis=1)
            per_head.append((qi, ks, _dot_nt(ql, kl), _dot_nt(qd, kd), rf[18:19, hs]))
        return per_head

    def outputs(ci, per_head, state):
        rs = slice(ci * c, (ci + 1) * c)
        a_bf = [jnp.where(diag, a_dg, a_off).astype(BF16) for _, _, a_off, a_dg, _ in per_head]
        outs, new_state = [], []
        for h in range(heads):
            hs = slice(h * HEAD, (h + 1) * HEAD)
            qi, ks, _, _, dec = per_head[h]
            v_h = v_ref[rs, hs]
            outs.append(_dot_nt(qi, state[h].astype(BF16))
                        + jnp.dot(a_bf[h], v_h, preferred_element_type=F32))
            new_state.append(dec * state[h] + _dot_tn(v_h, ks))
        for h in range(heads):
            hs = slice(h * HEAD, (h + 1) * HEAD)
            if finalize:
                ot = outs[h] + ob_ref[rs, hs]
                y = ot * lax.rsqrt(jnp.mean(ot * ot, axis=-1, keepdims=True) + EPS)
                y = y * gain_ref[:, hs]
                o_ref[rs, hs] = (y * _silu(hg_ref[rs, hs].astype(F32))).astype(o_ref.dtype)
            else:
                o_ref[rs, hs] = outs[h]
        return new_state

    state = [st_ref[h] for h in range(heads)]
    order = list(range(chunks_per_step - 1, -1, -1) if reverse else range(chunks_per_step))
    pending = scores(order[0])
    for n, ci in enumerate(order):
        upcoming = scores(order[n + 1]) if n + 1 < len(order) else None
        state = outputs(ci, pending, state)
        pending = upcoming
    for h in range(heads):
        st_ref[h] = state[h]


def _hgrn_pass(p_lat, p_ctx, lb_logits, o_back, gain, batch, seq, n_ctx, *, reverse):
    finalize = o_back is not None
    heads = 16
    hw = heads * HEAD
    cps = 8
    assert all(off % hw == 0 for off in (P_HQ, P_HFF, P_HFB, P_HI, P_HG, C_HFF, C_HFB, C_HI))
    tb = cps * CHUNK
    steps = seq // tb
    hblocks = HG_HEADS // heads
    dm = jnp.asarray(_hgrn_decay_matrix(reverse), BF16)

    def rblk(b, t):
        return b * steps + ((steps - 1 - t) if reverse else t)

    def lat(col0):
        return pl.BlockSpec((tb, hw), lambda b, h, t: (rblk(b, t), col0 // hw + h))

    z_lat, z_ctx = (P_HFB, C_HFB) if reverse else (P_HFF, C_HFF)
    in_specs = [pl.BlockSpec((2, hw), lambda b, h, t: (0, h)),
                pl.BlockSpec(dm.shape, lambda b, h, t: (0, 0)),
                pl.BlockSpec((n_ctx, hw), lambda b, h, t: (b, z_ctx // hw + h)),
                pl.BlockSpec((n_ctx, hw), lambda b, h, t: (b, C_HI // hw + h)),
                lat(P_HQ), lat(z_lat), lat(P_HI)]
    args = [lb_logits, dm, p_ctx, p_ctx, p_lat, p_lat, p_lat]
    if finalize:
        in_specs += [lat(P_HG), lat(0), pl.BlockSpec((1, hw), lambda b, h, t: (0, h))]
        args += [p_lat, o_back, gain]
    return pl.pallas_call(
        functools.partial(_hgrn_kernel, reverse=reverse, finalize=finalize, heads=heads,
                          chunks_per_step=cps),
        grid=(batch, hblocks, steps),
        in_specs=in_specs,
        out_specs=lat(0),
        out_shape=jax.ShapeDtypeStruct((batch * seq, HG_HEADS * HEAD), BF16 if finalize else F32),
        scratch_shapes=[pltpu.VMEM((heads, HEAD, HEAD), F32)],
        compiler_params=pltpu.CompilerParams(
            dimension_semantics=("arbitrary", "arbitrary", "arbitrary"),
            vmem_limit_bytes=VMEM_LIMIT),
        name="hgrn_fwd" if finalize else "hgrn_bwd",
    )(*args)


def _merge_kernel(hg_ref, at_ref, wh_ref, wa_ref, ma_ref, mb_ref, bm_ref, o_ref):
    half = hg_ref.shape[0] // 2
    halves = [slice(0, half), slice(half, 2 * half)]
    prods = [(jnp.dot(hg_ref[r, :], wh_ref[...], preferred_element_type=F32),
              jnp.dot(at_ref[r, :], wa_ref[...], preferred_element_type=F32)) for r in halves]
    for r, (ya, yb) in zip(halves, prods):
        ga = jax.nn.sigmoid(ma_ref[r, :].astype(F32) + bm_ref[0:1, :])
        gb = jax.nn.sigmoid(mb_ref[r, :].astype(F32) + bm_ref[1:2, :])
        o_ref[r, :] = (ga * ya + gb * yb).astype(o_ref.dtype)


def _merge(hg_branch, att_branch, wh_bf, wa_bf, p_lat, b_merge):
    m = hg_branch.shape[0]
    tm, tn = 1024, 512
    return pl.pallas_call(
        _merge_kernel,
        grid=(m // tm, D_MODEL // tn),
        in_specs=[pl.BlockSpec((tm, D_MODEL), lambda i, j: (i, 0)),
                  pl.BlockSpec((tm, D_MODEL), lambda i, j: (i, 0)),
                  pl.BlockSpec((D_MODEL, tn), lambda i, j: (0, j)),
                  pl.BlockSpec((D_MODEL, tn), lambda i, j: (0, j)),
                  pl.BlockSpec((tm, tn), lambda i, j: (i, P_MA // tn + j)),
                  pl.BlockSpec((tm, tn), lambda i, j: (i, P_MB // tn + j)),
                  pl.BlockSpec((2, tn), lambda i, j: (0, j))],
        out_specs=pl.BlockSpec((tm, tn), lambda i, j: (i, j)),
        out_shape=jax.ShapeDtypeStruct((m, D_MODEL), BF16),
        compiler_params=pltpu.CompilerParams(
            dimension_semantics=("arbitrary", "arbitrary"), vmem_limit_bytes=VMEM_LIMIT),
        name="merge",
    )(hg_branch, att_branch, wh_bf, wa_bf, p_lat, p_lat, b_merge)


def _out_kernel(y_ref, w_ref, x_ref, mod_ref, gain_ref, o_ref):
    half = y_ref.shape[0] // 2
    halves = [slice(0, half), slice(half, 2 * half)]
    zs = [jnp.dot(y_ref[r, :], w_ref[...], preferred_element_type=F32) for r in halves]
    gate = mod_ref[0, :, 2 * D_MODEL:3 * D_MODEL]
    for r, z in zip(halves, zs):
        h = x_ref[r, :] + gate * z
        y = h * lax.rsqrt(jnp.mean(h * h, axis=-1, keepdims=True) + EPS)
        o_ref[r, :] = y * gain_ref[...]


def _out(y, wo_bf, x2, mod3, fgain, seq):
    m = y.shape[0]
    tm = 512
    tiles_per_seq = seq // tm
    return pl.pallas_call(
        _out_kernel,
        grid=(m // tm,),
        in_specs=[pl.BlockSpec((tm, D_MODEL), lambda i: (i, 0)),
                  pl.BlockSpec((D_MODEL, D_MODEL), lambda i: (0, 0)),
                  pl.BlockSpec((tm, D_MODEL), lambda i: (i, 0)),
                  pl.BlockSpec((1, 1, 3 * D_MODEL), lambda i: (i // tiles_per_seq, 0, 0)),
                  pl.BlockSpec((1, D_MODEL), lambda i: (0, 0))],
        out_specs=pl.BlockSpec((tm, D_MODEL), lambda i: (i, 0)),
        out_shape=jax.ShapeDtypeStruct((m, D_MODEL), F32),
        compiler_params=pltpu.CompilerParams(
            dimension_semantics=("arbitrary",), vmem_limit_bytes=VMEM_LIMIT),
        name="out",
    )(y, wo_bf, x2, mod3, fgain)


def _rope_tables(seq):
    rows = seq // GRID_W
    row = jnp.repeat(jnp.arange(rows, dtype=F32), GRID_W)
    col = jnp.tile(jnp.arange(GRID_W, dtype=F32), rows)
    half = HEAD // 2
    inv_freq = 1.0 / (ROPE_BASE ** (jnp.arange(0, half, 2, dtype=F32) / half))
    ang_r = row[:, None] * inv_freq[None, :]
    ang_c = col[:, None] * inv_freq[None, :]
    ang = jnp.concatenate([ang_r, ang_c, ang_r, ang_c], axis=-1)
    sign = jnp.where(jnp.arange(HEAD) < half, -1.0, 1.0)
    return jnp.cos(ang), jnp.sin(ang) * sign


def kernel(x, c, ctx, c_ctx, w_ada, b_ada, norm_gain, w_in, b_merge, lb_logits_fwd,
           lb_logits_bwd, hgrn_norm_gain, w_o_hgrn, sink_logits, w_o_attn, w_out,
           final_norm_gain):
    batch, seq, d = x.shape
    n_ctx = ctx.shape[1]
    assert d == D_MODEL and w_ada.shape[0] == 1 and seq % 1024 == 0 and batch * n_ctx == 1024

    c8 = jnp.concatenate([c, c_ctx[None, :], jnp.zeros((8 - batch - 1, d), F32)], axis=0)
    mod8 = _ada(c8, w_ada[0], b_ada[0][None, :])
    mod3 = mod8[:batch, None, :]
    modc3 = mod8[batch:batch + 1, None, :]

    cos, sin = _rope_tables(seq)
    gain = norm_gain[0][None, :]
    x2 = x.reshape(batch * seq, d)
    w_bf, p_ctx = _wcast_ctx(ctx.reshape(batch * n_ctx, d), modc3, gain, w_in[0])
    p_lat = _inproj_latent(x2, mod3, gain, cos, sin, w_bf, seq)

    att_branch = _attention(p_lat, p_ctx, sink_logits[0][None, :], batch, seq, n_ctx)

    o_back = _hgrn_pass(p_lat, p_ctx, lb_logits_bwd, None, None, batch, seq, n_ctx, reverse=True)
    hg_branch = _hgrn_pass(p_lat, p_ctx, lb_logits_fwd, o_back, hgrn_norm_gain[0][None, :],
                           batch, seq, n_ctx, reverse=False)

    y = _merge(hg_branch, att_branch, w_o_hgrn[0].astype(BF16), w_o_attn[0].astype(BF16),
               p_lat, b_merge[0])
    out = _out(y, w_out[0].astype(BF16), x2, mod3, final_norm_gain[None, :], seq)
    return out.reshape(batch, seq, d)
```
